```python
import jax, jax.numpy as jnp
from jax import lax
import numpy as np


D_MODEL = 1024
BATCH = 8
SEQ = 4096
DEPTH = 4

HEAD_DIM = 64
N_HEADS = D_MODEL // HEAD_DIM
MIX_WIDTH = N_HEADS * HEAD_DIM
KV_HEADS_A = max(1, N_HEADS // 8)
KV_HEADS_B = max(1, N_HEADS // 4)
KV_HEADS_C = max(1, N_HEADS // 4)
ROT_DIM = HEAD_DIM // 4
ROPE_THETA = 500000.0
WINDOW_A = 128
DILATED_GROUPS = ((128, 1), (512, 4), (2048, 16))
BAND_BLOCK = 128
MOBA_BLOCK = 256
MOBA_TOPK = 3
MOBA_QCHUNK = 128
N_MIXERS = 3
DEEPNORM_ALPHA = (2 * DEPTH) ** 0.25
DEEPNORM_BETA = (8 * DEPTH) ** -0.25
LN_EPS = 1e-5
ATTN_SCALE = HEAD_DIM ** -0.5

kernel_name = 'hybrid_swa_dilated_moba_deepnorm'


def _in_layout(kind):
    q_cols = N_HEADS * HEAD_DIM
    if kind == 0:
        kv = KV_HEADS_A * HEAD_DIM
        return [(q_cols, False), (kv, False), (kv, True), (MIX_WIDTH, False)]
    if kind == 1:
        kv = KV_HEADS_B * HEAD_DIM
        parts = []
        for _ in DILATED_GROUPS:
            parts += [(q_cols, False), (kv, False), (kv, True)]
        return parts + [(MIX_WIDTH, False)]
    kv = KV_HEADS_C * HEAD_DIM
    return [(q_cols, False), (kv, False), (kv, True), (MIX_WIDTH, False)]


def _split(h, kind):
    sizes = [c for c, _ in _in_layout(kind)]
    offs = [int(o) for o in np.cumsum(sizes)[:-1]]
    return jnp.split(h, offs, axis=-1)


def _heads(t, n):
    return t.reshape(t.shape[0], t.shape[1], n, HEAD_DIM)


def _rope_tables(seq):
    inv = ROPE_THETA ** (-jnp.arange(0, ROT_DIM, 2, dtype=jnp.float32) / ROT_DIM)
    ang = jnp.arange(seq, dtype=jnp.float32)[:, None] * inv[None, :]
    return jnp.cos(ang), jnp.sin(ang)


def _partial_rope(t, cos, sin):
    half = ROT_DIM // 2
    tr = t[..., :ROT_DIM].astype(jnp.float32)
    t1, t2 = tr[..., :half], tr[..., half:]
    c = cos[None, :, None, :]
    s = sin[None, :, None, :]
    rot = jnp.concatenate([t1 * c - t2 * s, t2 * c + t1 * s], axis=-1).astype(t.dtype)
    return jnp.concatenate([rot, t[..., ROT_DIM:]], axis=-1)


def _layer_norm(t, g, b):
    tf = t.astype(jnp.float32)
    mu = tf.mean(-1, keepdims=True)
    var = jnp.square(tf - mu).mean(-1, keepdims=True)
    return ((tf - mu) * lax.rsqrt(var + LN_EPS) * g.astype(jnp.float32) + b.astype(jnp.float32)).astype(t.dtype)


def _banded_attention(q, k, v, max_dist, sink=None):
    nbat, L, H, hd = q.shape
    kvh = k.shape[2]
    g = H // kvh
    nb = L // BAND_BLOCK
    qb = q.reshape(nbat, nb, BAND_BLOCK, kvh, g, hd)

    def with_prev(t):
        tb = t.reshape(nbat, nb, BAND_BLOCK, kvh, hd)
        prev = jnp.pad(tb, ((0, 0), (1, 0), (0, 0), (0, 0), (0, 0)))[:, :-1]
        return jnp.concatenate([prev, tb], axis=2)

    kk, vv = with_prev(k), with_prev(v)
    s = jnp.einsum('bnqkgd,bnskd->bnkgqs', qb, kk).astype(jnp.float32) * ATTN_SCALE
    qpos = jnp.arange(nb)[:, None, None] * BAND_BLOCK + jnp.arange(BAND_BLOCK)[None, :, None]
    kpos = jnp.arange(nb)[:, None, None] * BAND_BLOCK - BAND_BLOCK + jnp.arange(2 * BAND_BLOCK)[None, None, :]
    dist = qpos - kpos
    mask = (dist >= 0) & (dist <= max_dist) & (kpos >= 0)
    s = jnp.where(mask[None, :, None, None], s, -jnp.inf)
    m = s.max(-1)
    if sink is not None:
        sk = sink.astype(jnp.float32).reshape(kvh, g)[None, None, :, :, None]
        m = jnp.maximum(m, sk)
    p = jnp.exp(s - m[..., None])
    denom = p.sum(-1)
    if sink is not None:
        denom = denom + jnp.exp(sk - m)
    o = jnp.einsum('bnkgqs,bnskd->bnqkgd', p.astype(v.dtype), vv).astype(jnp.float32)
    o = o / denom.transpose(0, 1, 4, 2, 3)[..., None]
    lse = (m + jnp.log(denom)).transpose(0, 1, 4, 2, 3)
    return o.reshape(nbat, L, H, hd), lse.reshape(nbat, L, H)


def _dilated_branch(q, k, v, window, dil):
    nbat, S = q.shape[0], q.shape[1]
    unit = dil * BAND_BLOCK
    s_pad = -(-S // unit) * unit
    L = s_pad // dil

    def fold(t):
        t = jnp.pad(t, ((0, 0), (0, s_pad - S), (0, 0), (0, 0)))
        nh = t.shape[2]
        t = t.reshape(nbat, L, dil, nh, HEAD_DIM).transpose(0, 2, 1, 3, 4)
        return t.reshape(nbat * dil, L, nh, HEAD_DIM)

    o, lse = _banded_attention(fold(q), fold(k), fold(v), window // dil)
    o = o.reshape(nbat, dil, L, N_HEADS, HEAD_DIM).transpose(0, 2, 1, 3, 4).reshape(nbat, s_pad, N_HEADS, HEAD_DIM)
    lse = lse.reshape(nbat, dil, L, N_HEADS).transpose(0, 2, 1, 3).reshape(nbat, s_pad, N_HEADS)
    return o[:, :S], lse[:, :S]


def _moba_attention(q, k, v):
    nbat, S, H, hd = q.shape
    kvh = k.shape[2]
    g = H // kvh
    s_pad = -(-S // MOBA_BLOCK) * MOBA_BLOCK
    padw = ((0, 0), (0, s_pad - S), (0, 0), (0, 0))
    q, k, v = jnp.pad(q, padw), jnp.pad(k, padw), jnp.pad(v, padw)
    nb = s_pad // MOBA_BLOCK
    kb = k.reshape(nbat, nb, MOBA_BLOCK, kvh, hd)
    vb = v.reshape(nbat, nb, MOBA_BLOCK, kvh, hd)
    kmean = kb.astype(jnp.float32).mean(axis=2)
    qg = q.reshape(nbat, s_pad, kvh, g, hd)
    gate = jnp.einsum('bskgd,bnkd->bskgn', qg.astype(jnp.float32), kmean)
    own = jnp.arange(s_pad) // MOBA_BLOCK
    past = jnp.arange(nb)[None, :] < own[:, None]
    gate = jnp.where(past[None, :, None, None, :], gate, -jnp.inf)
    n_top = min(MOBA_TOPK, nb)
    _, sel = lax.top_k(gate, n_top)
    valid = sel < own[None, :, None, None, None]
    nq = s_pad // MOBA_QCHUNK
    kbt = kb.transpose(0, 3, 1, 2, 4)
    vbt = vb.transpose(0, 3, 1, 2, 4)
    hidx = jnp.arange(kvh)[None, :, None, None]

    def chunk(args):
        qc, selc, validc, bi, ci = args
        kt, vt = kbt[bi], vbt[bi]
        ks = kt[hidx, selc]
        vs = vt[hidx, selc]
        ob = (ci * MOBA_QCHUNK) // MOBA_BLOCK
        ko = lax.dynamic_index_in_dim(kt, ob, axis=1, keepdims=False)
        vo = lax.dynamic_index_in_dim(vt, ob, axis=1, keepdims=False)
        s_sel = jnp.einsum('qkgd,qkgtmd->qkgtm', qc, ks).astype(jnp.float32) * ATTN_SCALE
        s_sel = jnp.where(validc[..., None], s_sel, -jnp.inf)
        s_own = jnp.einsum('qkgd,kmd->qkgm', qc, ko).astype(jnp.float32) * ATTN_SCALE
        qpos = ci * MOBA_QCHUNK + jnp.arange(MOBA_QCHUNK)
        kpos = ob * MOBA_BLOCK + jnp.arange(MOBA_BLOCK)
        s_own = jnp.where((kpos[None, :] <= qpos[:, None])[:, None, None, :], s_own, -jnp.inf)
        m = jnp.maximum(s_sel.max((-2, -1)), s_own.max(-1))
        p_sel = jnp.exp(s_sel - m[..., None, None])
        p_own = jnp.exp(s_own - m[..., None])
        denom = p_sel.sum((-2, -1)) + p_own.sum(-1)
        o = (jnp.einsum('qkgtm,qkgtmd->qkgd', p_sel.astype(v.dtype), vs).astype(jnp.float32)
             + jnp.einsum('qkgm,kmd->qkgd', p_own.astype(v.dtype), vo).astype(jnp.float32))
        return o / denom[..., None]

    xs = (qg.reshape(nbat * nq, MOBA_QCHUNK, kvh, g, hd),
          sel.reshape(nbat * nq, MOBA_QCHUNK, kvh, g, n_top),
          valid.reshape(nbat * nq, MOBA_QCHUNK, kvh, g, n_top),
          jnp.repeat(jnp.arange(nbat, dtype=jnp.int32), nq),
          jnp.tile(jnp.arange(nq, dtype=jnp.int32), nbat))
    out = lax.map(chunk, xs)
    return out.reshape(nbat, s_pad, H, hd)[:, :S]


def _mixer_a(h, sink, cos, sin):
    q, k, v, z = _split(h, 0)
    q = _partial_rope(_heads(q, N_HEADS), cos, sin)
    k = _partial_rope(_heads(k, KV_HEADS_A), cos, sin)
    o, _ = _banded_attention(q, k, _heads(v, KV_HEADS_A), WINDOW_A - 1, sink)
    return o.reshape(h.shape[0], h.shape[1], MIX_WIDTH).astype(h.dtype), z


def _mixer_b(h, cos, sin):
    parts = _split(h, 1)
    z = parts[-1]
    outs, lses = [], []
    for gi, (window, dil) in enumerate(DILATED_GROUPS):
        q, k, v = parts[3 * gi], parts[3 * gi + 1], parts[3 * gi + 2]
        q = _partial_rope(_heads(q, N_HEADS), cos, sin)
        k = _partial_rope(_heads(k, KV_HEADS_B), cos, sin)
        o, lse = _dilated_branch(q, k, _heads(v, KV_HEADS_B), window, dil)
        outs.append(o)
        lses.append(lse)
    w = jax.nn.softmax(jnp.stack(lses, axis=0), axis=0)
    o = (w[..., None] * jnp.stack(outs, axis=0)).sum(0)
    return o.reshape(h.shape[0], h.shape[1], MIX_WIDTH).astype(h.dtype), z


def _mixer_c(h, cos, sin):
    q, k, v, z = _split(h, 2)
    q = _partial_rope(_heads(q, N_HEADS), cos, sin)
    k = _partial_rope(_heads(k, KV_HEADS_C), cos, sin)
    o = _moba_attention(q, k, _heads(v, KV_HEADS_C))
    return o.reshape(h.shape[0], h.shape[1], MIX_WIDTH).astype(h.dtype), z


def _make_w_in(kit, kind):
    parts = []
    for cols, is_v in _in_layout(kind):
        scale = (D_MODEL ** -0.5) * (DEEPNORM_BETA if is_v else 1.0)
        parts.append(jax.random.normal(next(kit), (D_MODEL, cols), jnp.float32) * scale)
    return jnp.concatenate(parts, axis=1)


def setup_inputs(seed: int = 0) -> dict:
    key = jax.random.key(seed)
    kit = iter(jax.random.split(key, 64))
    out = {'x': jax.random.normal(next(kit), (BATCH, SEQ, D_MODEL), jnp.float32)}
    for i in range(DEPTH):
        kind = i % N_MIXERS
        out['w_in_%d' % i] = _make_w_in(kit, kind)
        if kind == 0:
            out['sink_%d' % i] = jax.random.normal(next(kit), (N_HEADS,), jnp.float32) * 0.5
        out['w_out_%d' % i] = jax.random.normal(next(kit), (MIX_WIDTH, D_MODEL), jnp.float32) * (MIX_WIDTH ** -0.5) * DEEPNORM_BETA
        out['ln_g_%d' % i] = 1.0 + 0.01 * jax.random.normal(next(kit), (D_MODEL,), jnp.float32)
        out['ln_b_%d' % i] = 0.01 * jax.random.normal(next(kit), (D_MODEL,), jnp.float32)
    return out


def reference(x, w_in_0, sink_0, w_out_0, ln_g_0, ln_b_0,
              w_in_1, w_out_1, ln_g_1, ln_b_1,
              w_in_2, w_out_2, ln_g_2, ln_b_2,
              w_in_3, sink_3, w_out_3, ln_g_3, ln_b_3):
    layers = [(w_in_0, sink_0, w_out_0, ln_g_0, ln_b_0),
              (w_in_1, None, w_out_1, ln_g_1, ln_b_1),
              (w_in_2, None, w_out_2, ln_g_2, ln_b_2),
              (w_in_3, sink_3, w_out_3, ln_g_3, ln_b_3)]
    cos, sin = _rope_tables(x.shape[1])
    for i in range(DEPTH):
        w_in, sink, w_out, ln_g, ln_b = layers[i]
        kind = i % N_MIXERS
        h = x @ w_in
        if kind == 0:
            mix, z = _mixer_a(h, sink, cos, sin)
        elif kind == 1:
            mix, z = _mixer_b(h, cos, sin)
        else:
            mix, z = _mixer_c(h, cos, sin)
        y = mix * jax.nn.silu(z)
        x = _layer_norm(DEEPNORM_ALPHA * x + y @ w_out, ln_g, ln_b)
    return x
```

```python
import functools

import numpy as np
import jax
import jax.numpy as jnp
from jax import lax
from jax.experimental import pallas as pl
from jax.experimental.pallas import tpu as pltpu

D_MODEL = 1024
HEAD_DIM = 64
N_HEADS = D_MODEL // HEAD_DIM
ROT_DIM = HEAD_DIM // 4
ROPE_THETA = 500000.0
DEPTH = 4
N_MIXERS = 3
KV_HEADS = (2, 4, 4)
WINDOW_A = 128
DILATED_GROUPS = ((128, 1), (512, 4), (2048, 16))
BAND = 128
MOBA_BLOCK = 256
MOBA_TOPK = 3
DEEPNORM_ALPHA = (2 * DEPTH) ** 0.25
LN_EPS = 1e-5
ATTN_SCALE = HEAD_DIM ** -0.5

LANES = 128
NEG_BIG = -1e30
VMEM_CAP = 60000 * 1024

F32 = jnp.float32
BF16 = jnp.bfloat16
NT_DIMS = (((1,), (1,)), ((), ()))


def _params(vmem_bytes, n_grid):
    limit = int(min(max(2 * vmem_bytes, 32 * 1024 * 1024), VMEM_CAP))
    return pltpu.CompilerParams(dimension_semantics=("arbitrary",) * n_grid,
                                vmem_limit_bytes=limit)


def _silu(z):
    return z / (1.0 + jnp.exp(-z))


def _inproj_kernel(x_ref, rope_ref, wq_ref, wk_ref, wv_ref, wz_ref,
                   q_ref, kx_ref, vx_ref, z_ref, *, n_kvh, moba, seq_tiles):
    tm = x_ref.shape[0]
    xb = x_ref[...].astype(BF16)
    cs, sa, sb = rope_ref[0], rope_ref[1], rope_ref[2]
    lane = lax.broadcasted_iota(jnp.int32, (tm, LANES), 1)
    lo = lane < HEAD_DIM

    def rope(t):
        return (t * cs + pltpu.roll(t, LANES - ROT_DIM // 2, 1) * sa
                + pltpu.roll(t, ROT_DIM // 2, 1) * sb)

    for c in range(q_ref.shape[1] // 512):
        acc = jnp.dot(xb, wq_ref[:, c * 512:(c + 1) * 512], preferred_element_type=F32)
        for s in range(4):
            col = c * 512 + s * LANES
            q_ref[:, col:col + LANES] = rope(acc[:, s * LANES:(s + 1) * LANES]).astype(BF16)

    if moba:
        row = lax.broadcasted_iota(jnp.int32, (tm, LANES), 0)
        pos = (pl.program_id(0) % seq_tiles) * tm + row
        k_aux = jnp.where(lane == HEAD_DIM + lax.shift_right_logical(pos, 8), 1.0, 0.0)
    else:
        k_aux = jnp.zeros((tm, LANES), F32)
    acc_k = jnp.dot(xb, wk_ref[...], preferred_element_type=F32)
    acc_v = jnp.dot(xb, wv_ref[...], preferred_element_type=F32)
    for c in range(n_kvh // 2):
        k2 = rope(acc_k[:, c * LANES:(c + 1) * LANES])
        v2 = acc_v[:, c * LANES:(c + 1) * LANES]
        for e in range(2):
            col = (2 * c + e) * LANES
            ke = k2 if e == 0 else pltpu.roll(k2, HEAD_DIM, 1)
            ve = v2 if e == 0 else pltpu.roll(v2, HEAD_DIM, 1)
            kx_ref[:, col:col + LANES] = jnp.where(lo, ke, k_aux).astype(BF16)
            vx_ref[:, col:col + LANES] = jnp.where(lo, ve, 1.0).astype(BF16)

    for c in range(2):
        acc = jnp.dot(xb, wz_ref[:, c * 512:(c + 1) * 512], preferred_element_type=F32)
        z_ref[:, c * 512:(c + 1) * 512] = acc.astype(BF16)


def _inproj(x2, rope_tab, wq, wk, wv, wz, *, seq, moba):
    t_rows = x2.shape[0]
    tm = 512
    n_q = wq.shape[1]
    n_kvh = wk.shape[1] // HEAD_DIM
    seq_tiles = seq // tm
    full = lambda i: (0, 0)
    row = lambda i: (i, 0)
    vmem = (2 * tm * D_MODEL * 4 + 2 * 3 * tm * LANES * 4
            + 2 * 2 * D_MODEL * (n_q + 2 * wk.shape[1] + D_MODEL)
            + 2 * 2 * tm * (n_q + 2 * n_kvh * LANES + D_MODEL)
            + 4 * tm * 512 * 4)
    return pl.pallas_call(
        functools.partial(_inproj_kernel, n_kvh=n_kvh, moba=moba, seq_tiles=seq_tiles),
        grid=(t_rows // tm,),
        in_specs=[pl.BlockSpec((tm, D_MODEL), row),
                  pl.BlockSpec((3, tm, LANES), lambda i: (0, i % seq_tiles, 0)),
                  pl.BlockSpec(wq.shape, full), pl.BlockSpec(wk.shape, full),
                  pl.BlockSpec(wv.shape, full), pl.BlockSpec(wz.shape, full)],
        out_specs=[pl.BlockSpec((tm, n_q), row), pl.BlockSpec((tm, n_kvh * LANES), row),
                   pl.BlockSpec((tm, n_kvh * LANES), row), pl.BlockSpec((tm, D_MODEL), row)],
        out_shape=[jax.ShapeDtypeStruct((t_rows, n_q), BF16),
                   jax.ShapeDtypeStruct((t_rows, n_kvh * LANES), BF16),
                   jax.ShapeDtypeStruct((t_rows, n_kvh * LANES), BF16),
                   jax.ShapeDtypeStruct((t_rows, D_MODEL), BF16)],
        compiler_params=_params(vmem, 1),
        name="inproj",
    )(x2, rope_tab, wq, wk, wv, wz)


def _outproj_kernel(y_ref, x_ref, w_ref, g_ref, b_ref, o_ref):
    t = DEEPNORM_ALPHA * x_ref[...] + jnp.dot(y_ref[...], w_ref[...], preferred_element_type=F32)
    mu = jnp.mean(t, axis=-1, keepdims=True)
    d = t - mu
    var = jnp.mean(d * d, axis=-1, keepdims=True)
    o_ref[...] = d * lax.rsqrt(var + LN_EPS) * g_ref[...] + b_ref[...]


def _outproj(y2, x2, w, g, b):
    t_rows = x2.shape[0]
    tm = 512
    row = lambda i: (i, 0)
    full = lambda i: (0, 0)
    vmem = 2 * tm * D_MODEL * (2 + 4 + 4) + 2 * D_MODEL * D_MODEL * 2 + 3 * tm * D_MODEL * 4
    return pl.pallas_call(
        _outproj_kernel,
        grid=(t_rows // tm,),
        in_specs=[pl.BlockSpec((tm, D_MODEL), row), pl.BlockSpec((tm, D_MODEL), row),
                  pl.BlockSpec((D_MODEL, D_MODEL), full),
                  pl.BlockSpec((1, D_MODEL), full), pl.BlockSpec((1, D_MODEL), full)],
        out_specs=pl.BlockSpec((tm, D_MODEL), row),
        out_shape=jax.ShapeDtypeStruct((t_rows, D_MODEL), F32),
        compiler_params=_params(vmem, 1),
        name="outproj_ln",
    )(y2, x2, w, g, b)


def _band_mask(max_dist):
    i = np.arange(BAND)[:, None]
    c = np.arange(2 * BAND)[None, :]
    dist = BAND + i - c
    ok = (dist >= 0) & (dist <= max_dist)
    first = ok & (c >= BAND)
    return jnp.asarray(np.where(np.stack([first, ok]), 0.0, NEG_BIG), F32)


def _band_kernel(*refs, n_kvh, has_sink, tile_axis):
    if has_sink:
        mask_ref, sink_ref, q_ref, kc_ref, kp_ref, vc_ref, vp_ref, z_ref, y_ref, kbuf, vbuf = refs
    else:
        mask_ref, q_ref, kc_ref, kp_ref, vc_ref, vp_ref, o_ref, lse_ref, kbuf, vbuf = refs
    g = N_HEADS // n_kvh
    tq = q_ref.shape[0]
    i_tile = pl.program_id(tile_axis)
    kbuf[0:BAND, :] = kp_ref[...]
    kbuf[BAND:BAND + tq, :] = kc_ref[...]
    vbuf[0:BAND, :] = vp_ref[...]
    vbuf[BAND:BAND + tq, :] = vc_ref[...]
    lane = lax.broadcasted_iota(jnp.int32, (BAND, LANES), 1)
    lo = lane < HEAD_DIM

    def block(bb, carry):
        r0 = pl.multiple_of(bb * BAND, BAND)
        mask = mask_ref[jnp.where((i_tile == 0) & (bb == 0), 0, 1)]
        for j in range(n_kvh):
            kcat = kbuf[pl.ds(r0, 2 * BAND), j * LANES:(j + 1) * LANES]
            vcat = vbuf[pl.ds(r0, 2 * BAND), j * LANES:(j + 1) * LANES]
            rows = []
            for gi in range(g):
                h = j * g + gi
                slab = q_ref[pl.ds(r0, BAND), (h // 2) * LANES:(h // 2 + 1) * LANES].astype(F32)
                if h % 2:
                    slab = pltpu.roll(slab, HEAD_DIM, 1)
                rows.append(jnp.where(lo, slab, 0.0).astype(BF16))
            qrows = jnp.concatenate(rows, axis=0)
            s = lax.dot_general(qrows, kcat, NT_DIMS, preferred_element_type=F32)
            ps, ms, es = [], [], []
            for gi in range(g):
                sh = s[gi * BAND:(gi + 1) * BAND] + mask
                m = jnp.max(sh, axis=-1, keepdims=True)
                if has_sink:
                    sk = sink_ref[j * g + gi]
                    m = jnp.maximum(m, sk)
                    es.append(jnp.exp(sk - m))
                ms.append(m)
                ps.append(jnp.exp(sh - m).astype(BF16))
            p = jnp.concatenate(ps, axis=0)
            oext = jnp.dot(p, vcat, preferred_element_type=F32)
            for pi in range(g // 2):
                oe = oext[(2 * pi) * BAND:(2 * pi + 1) * BAND]
                oo = oext[(2 * pi + 1) * BAND:(2 * pi + 2) * BAND]
                de, do = oe, oo
                if has_sink:
                    de = oe + es[2 * pi]
                    do = oo + es[2 * pi + 1]
                o_slab = jnp.where(lo, oe / pltpu.roll(de, HEAD_DIM, 1),
                                   pltpu.roll(oo, HEAD_DIM, 1) / do)
                col = (j * g // 2 + pi) * LANES
                if has_sink:
                    zf = z_ref[pl.ds(r0, BAND), col:col + LANES].astype(F32)
                    y_ref[pl.ds(r0, BAND), col:col + LANES] = (o_slab * _silu(zf)).astype(BF16)
                else:
                    lse_e = ms[2 * pi] + jnp.log(pltpu.roll(oe, HEAD_DIM, 1))
                    lse_o = ms[2 * pi + 1] + jnp.log(oo)
                    o_ref[pl.ds(r0, BAND), col:col + LANES] = o_slab
                    lse_ref[pl.ds(r0, BAND), col:col + LANES] = jnp.where(lo, lse_e, lse_o)
        return carry

    lax.fori_loop(0, tq // BAND, block, 0)


def _swa_attention(q, kx, vx, z, sink, *, max_dist):
    n_b, seq, _ = q.shape
    n_kvh = kx.shape[2] // LANES
    tq = 512
    bpt = tq // BAND
    cur = lambda b, i: (b, i, 0)
    prev = lambda b, i: (b, jnp.maximum(i * bpt - 1, 0), 0)
    kvw = n_kvh * LANES
    vmem = 2 * tq * D_MODEL * 2 * 3 + 4 * 2 * (tq + BAND) * kvw * 2 + 2 * (tq + BAND) * kvw * 2
    return pl.pallas_call(
        functools.partial(_band_kernel, n_kvh=n_kvh, has_sink=True, tile_axis=1),
        grid=(n_b, seq // tq),
        in_specs=[pl.BlockSpec((2, BAND, 2 * BAND), lambda b, i: (0, 0, 0)),
                  pl.BlockSpec(memory_space=pltpu.SMEM),
                  pl.BlockSpec((None, tq, D_MODEL), cur),
                  pl.BlockSpec((None, tq, kvw), cur), pl.BlockSpec((None, BAND, kvw), prev),
                  pl.BlockSpec((None, tq, kvw), cur), pl.BlockSpec((None, BAND, kvw), prev),
                  pl.BlockSpec((None, tq, D_MODEL), cur)],
        out_specs=pl.BlockSpec((None, tq, D_MODEL), cur),
        out_shape=jax.ShapeDtypeStruct((n_b, seq, D_MODEL), BF16),
        scratch_shapes=[pltpu.VMEM((tq + BAND, kvw), BF16), pltpu.VMEM((tq + BAND, kvw), BF16)],
        compiler_params=_params(vmem, 2),
        name="swa_attention",
    )(_band_mask(max_dist), sink, q, kx, kx, vx, vx, z)


def _dilated_group(q, kx, vx, gi, *, dil, max_dist):
    n_b, seq, nq3 = q.shape
    n_grp = nq3 // D_MODEL
    kvw = kx.shape[2] // n_grp
    n_kvh = kvw // LANES
    fold = seq // dil
    tq = min(512, fold)
    bpt = tq // BAND
    qf = q.reshape(n_b, fold, dil * nq3)
    kf = kx.reshape(n_b, fold, dil * n_grp * kvw)
    vf = vx.reshape(n_b, fold, dil * n_grp * kvw)
    cur_in = lambda b, r, i: (b, i, r * n_grp + gi)
    prev_in = lambda b, r, i: (b, jnp.maximum(i * bpt - 1, 0), r * n_grp + gi)
    cur_out = lambda b, r, i: (b, i, r)
    vmem = 2 * tq * D_MODEL * (2 + 4 + 4) + 4 * 2 * (tq + BAND) * kvw * 2 + 2 * (tq + BAND) * kvw * 2
    o, lse = pl.pallas_call(
        functools.partial(_band_kernel, n_kvh=n_kvh, has_sink=False, tile_axis=2),
        grid=(n_b, dil, fold // tq),
        in_specs=[pl.BlockSpec((2, BAND, 2 * BAND), lambda b, r, i: (0, 0, 0)),
                  pl.BlockSpec((None, tq, D_MODEL), cur_in),
                  pl.BlockSpec((None, tq, kvw), cur_in), pl.BlockSpec((None, BAND, kvw), prev_in),
                  pl.BlockSpec((None, tq, kvw), cur_in), pl.BlockSpec((None, BAND, kvw), prev_in)],
        out_specs=[pl.BlockSpec((None, tq, D_MODEL), cur_out), pl.BlockSpec((None, tq, D_MODEL), cur_out)],
        out_shape=[jax.ShapeDtypeStruct((n_b, fold, dil * D_MODEL), F32),
                   jax.ShapeDtypeStruct((n_b, fold, dil * D_MODEL), F32)],
        scratch_shapes=[pltpu.VMEM((tq + BAND, kvw), BF16), pltpu.VMEM((tq + BAND, kvw), BF16)],
        compiler_params=_params(vmem, 3),
        name="dilated_group_%d" % gi,
    )(_band_mask(max_dist), qf, kf, kf, vf, vf)
    return o.reshape(n_b, seq, D_MODEL), lse.reshape(n_b, seq, D_MODEL)


def _dilated_merge_kernel(o0, o1, o2, l0, l1, l2, z_ref, y_ref):
    a, b, c = l0[...], l1[...], l2[...]
    m = jnp.maximum(jnp.maximum(a, b), c)
    ea, eb, ec = jnp.exp(a - m), jnp.exp(b - m), jnp.exp(c - m)
    o = (ea * o0[...] + eb * o1[...] + ec * o2[...]) / (ea + eb + ec)
    y_ref[...] = (o * _silu(z_ref[...].astype(F32))).astype(BF16)


def _dilated_merge(outs, lses, z2):
    t_rows = z2.shape[0]
    tm = 256
    row = lambda i: (i, 0)
    spec = pl.BlockSpec((tm, D_MODEL), row)
    vmem = 2 * tm * D_MODEL * (6 * 4 + 2 + 2) + 8 * tm * D_MODEL * 4
    return pl.pallas_call(
        _dilated_merge_kernel,
        grid=(t_rows // tm,),
        in_specs=[spec] * 7,
        out_specs=spec,
        out_shape=jax.ShapeDtypeStruct((t_rows, D_MODEL), BF16),
        compiler_params=_params(vmem, 1),
        name="dilated_merge",
    )(*[a.reshape(t_rows, D_MODEL) for a in outs], *[a.reshape(t_rows, D_MODEL) for a in lses], z2)


def _moba_kernel(tril_ref, q_ref, kx_ref, vx_ref, z_ref, y_ref,
                 km_scr, qext_scr, m_scr, acc_scr, *, n_blk):
    t = pl.program_id(2)
    g = 4
    tq = MOBA_BLOCK
    rows_n = g * tq
    lane = lax.broadcasted_iota(jnp.int32, (tq, LANES), 1)
    lo = lane < HEAD_DIM

    @pl.when(t == 0)
    def _block_means():
        means = []
        for n in range(n_blk):
            kb = kx_ref[n * MOBA_BLOCK:(n + 1) * MOBA_BLOCK, :].astype(F32)
            means.append(jnp.sum(kb, axis=0, keepdims=True) * (1.0 / MOBA_BLOCK))
        km = jnp.concatenate(means, axis=0)
        km = jnp.where(lax.broadcasted_iota(jnp.int32, km.shape, 1) < HEAD_DIM, km, 0.0)
        km_scr[...] = jnp.concatenate(
            [jnp.zeros((HEAD_DIM, LANES), F32), km,
             jnp.zeros((LANES - HEAD_DIM - n_blk, LANES), F32)], axis=0).astype(BF16)

    rows = []
    for gi in range(g):
        slab = q_ref[:, (gi // 2) * LANES:(gi // 2 + 1) * LANES].astype(F32)
        if gi % 2:
            slab = pltpu.roll(slab, HEAD_DIM, 1)
        rows.append(jnp.where(lo, slab, 0.0))
    qrows = jnp.concatenate(rows, axis=0)

    gate_t = lax.dot_general(km_scr[...], qrows.astype(BF16), NT_DIMS, preferred_element_type=F32)
    gate = gate_t[HEAD_DIM:HEAD_DIM + n_blk]
    blk = lax.broadcasted_iota(jnp.int32, gate.shape, 0)
    gate = jnp.where(blk < t, gate, -jnp.inf)
    sel = jnp.zeros(gate.shape, F32)
    for _ in range(MOBA_TOPK):
        mx = jnp.max(gate, axis=0, keepdims=True)
        is_max = (gate == mx) & (mx > -jnp.inf)
        first = jnp.min(jnp.where(is_max, blk, n_blk), axis=0, keepdims=True)
        pick = blk == first
        sel = jnp.where(pick, 1.0, sel)
        gate = jnp.where(pick, -jnp.inf, gate)
    bias_t = jnp.where((sel > 0.0) | (blk == t), 0.0, NEG_BIG)
    bias_full = jnp.concatenate(
        [jnp.zeros((HEAD_DIM, rows_n), F32), bias_t,
         jnp.zeros((LANES - HEAD_DIM - n_blk, rows_n), F32)], axis=0)
    bias_rows = bias_full.T
    lane_r = lax.broadcasted_iota(jnp.int32, (rows_n, LANES), 1)
    qext = jnp.where(lane_r < HEAD_DIM, qrows, bias_rows).astype(BF16)
    qext_scr[...] = qext

    r_own = pl.multiple_of(t * MOBA_BLOCK, MOBA_BLOCK)
    s = lax.dot_general(qext, kx_ref[pl.ds(r_own, MOBA_BLOCK), :], NT_DIMS, preferred_element_type=F32)
    tril = tril_ref[...]
    s = jnp.concatenate([s[gi * tq:(gi + 1) * tq] + tril for gi in range(g)], axis=0)
    m0 = jnp.max(s, axis=-1, keepdims=True)
    p = jnp.exp(s - m0).astype(BF16)
    acc_scr[...] = jnp.dot(p, vx_ref[pl.ds(r_own, MOBA_BLOCK), :], preferred_element_type=F32)
    m_scr[...] = jnp.broadcast_to(m0, (rows_n, LANES))

    def past_block(n, carry):
        r0 = pl.multiple_of(n * MOBA_BLOCK, MOBA_BLOCK)
        sn = lax.dot_general(qext_scr[...], kx_ref[pl.ds(r0, MOBA_BLOCK), :], NT_DIMS,
                             preferred_element_type=F32)
        m_old = m_scr[...]
        m_new = jnp.maximum(m_old, jnp.max(sn, axis=-1, keepdims=True))
        alpha = jnp.exp(m_old - m_new)
        pn = jnp.exp(sn - jnp.concatenate([m_new, m_new], axis=1)).astype(BF16)
        acc_scr[...] = alpha * acc_scr[...] + jnp.dot(pn, vx_ref[pl.ds(r0, MOBA_BLOCK), :],
                                                      preferred_element_type=F32)
        m_scr[...] = m_new
        return carry

    lax.fori_loop(0, t, past_block, 0)

    acc = acc_scr[...]
    for pi in range(g // 2):
        oe = acc[(2 * pi) * tq:(2 * pi + 1) * tq]
        oo = acc[(2 * pi + 1) * tq:(2 * pi + 2) * tq]
        o_slab = jnp.where(lo, oe / pltpu.roll(oe, HEAD_DIM, 1), pltpu.roll(oo, HEAD_DIM, 1) / oo)
        zf = z_ref[:, pi * LANES:(pi + 1) * LANES].astype(F32)
        y_ref[:, pi * LANES:(pi + 1) * LANES] = (o_slab * _silu(zf)).astype(BF16)


def _moba_attention(q, kx, vx, z):
    n_b, seq, _ = q.shape
    n_kvh = kx.shape[2] // LANES
    n_blk = seq // MOBA_BLOCK
    g = N_HEADS // n_kvh
    tq = MOBA_BLOCK
    qw = g * HEAD_DIM
    tril = jnp.asarray(np.where(np.tril(np.ones((tq, tq), bool)), 0.0, NEG_BIG), F32)
    tile = lambda b, j, t: (b, t, j)
    whole = lambda b, j, t: (b, 0, j)
    vmem = (2 * 2 * seq * LANES * 2 + 2 * 3 * tq * qw * 2 + tq * tq * 4 * 2
            + g * tq * LANES * (2 + 4 + 4) + 6 * g * tq * tq * 4)
    return pl.pallas_call(
        functools.partial(_moba_kernel, n_blk=n_blk),
        grid=(n_b, n_kvh, seq // tq),
        in_specs=[pl.BlockSpec((tq, tq), lambda b, j, t: (0, 0)),
                  pl.BlockSpec((None, tq, qw), tile),
                  pl.BlockSpec((None, seq, LANES), whole), pl.BlockSpec((None, seq, LANES), whole),
                  pl.BlockSpec((None, tq, qw), tile)],
        out_specs=pl.BlockSpec((None, tq, qw), tile),
        out_shape=jax.ShapeDtypeStruct((n_b, seq, D_MODEL), BF16),
        scratch_shapes=[pltpu.VMEM((LANES, LANES), BF16), pltpu.VMEM((g * tq, LANES), BF16),
                        pltpu.VMEM((g * tq, LANES), F32), pltpu.VMEM((g * tq, LANES), F32)],
        compiler_params=_params(vmem, 3),
        name="moba_attention",
    )(tril, q, kx, vx, z)


def _rope_table(seq):
    half = ROT_DIM // 2
    inv = ROPE_THETA ** (-jnp.arange(0, ROT_DIM, 2, dtype=F32) / ROT_DIM)
    ang = jnp.arange(seq, dtype=F32)[:, None] * inv[None, :]
    cos, sin = jnp.cos(ang), jnp.sin(ang)
    rest = HEAD_DIM - ROT_DIM
    zero_h = jnp.zeros((seq, half), F32)
    c64 = jnp.concatenate([cos, cos, jnp.ones((seq, rest), F32)], axis=1)
    a64 = jnp.concatenate([-sin, zero_h, jnp.zeros((seq, rest), F32)], axis=1)
    b64 = jnp.concatenate([zero_h, sin, jnp.zeros((seq, rest), F32)], axis=1)
    return jnp.stack([jnp.concatenate([t, t], axis=1) for t in (c64, a64, b64)])


def _split_w_in(w_in, kind):
    kv = KV_HEADS[kind] * HEAD_DIM
    n_grp = len(DILATED_GROUPS) if kind == 1 else 1
    qs, ks, vs = [], [], []
    off = 0
    for _ in range(n_grp):
        qs.append(w_in[:, off:off + D_MODEL]); off += D_MODEL
        ks.append(w_in[:, off:off + kv]); off += kv
        vs.append(w_in[:, off:off + kv]); off += kv
    wz = w_in[:, off:off + D_MODEL]
    cat = lambda parts: jnp.concatenate(parts, axis=1)
    return ((cat(qs) * ATTN_SCALE).astype(BF16), cat(ks).astype(BF16),
            cat(vs).astype(BF16), wz.astype(BF16))


def kernel(x, w_in_0, sink_0, w_out_0, ln_g_0, ln_b_0, w_in_1, w_out_1, ln_g_1, ln_b_1, w_in_2, w_out_2, ln_g_2, ln_b_2, w_in_3, sink_3, w_out_3, ln_g_3, ln_b_3):
    n_b, seq, d = x.shape
    assert d == D_MODEL and seq % max(dl * BAND for _, dl in DILATED_GROUPS) == 0
    assert (seq // MOBA_BLOCK) % 8 == 0 and seq // MOBA_BLOCK <= LANES - HEAD_DIM
    layers = [(w_in_0, sink_0, w_out_0, ln_g_0, ln_b_0),
              (w_in_1, None, w_out_1, ln_g_1, ln_b_1),
              (w_in_2, None, w_out_2, ln_g_2, ln_b_2),
              (w_in_3, sink_3, w_out_3, ln_g_3, ln_b_3)]
    rope_tab = _rope_table(seq)
    x2 = x.reshape(n_b * seq, D_MODEL)
    for i, (w_in, sink, w_out, ln_g, ln_b) in enumerate(layers):
        kind = i % N_MIXERS
        wq, wk, wv, wz = _split_w_in(w_in, kind)
        q, kx, vx, z = _inproj(x2, rope_tab, wq, wk, wv, wz, seq=seq, moba=(kind == 2))
        b3 = lambda a: a.reshape(n_b, seq, a.shape[-1])
        if kind == 0:
            y = _swa_attention(b3(q), b3(kx), b3(vx), b3(z), sink.astype(F32), max_dist=WINDOW_A - 1)
        elif kind == 1:
            outs, lses = [], []
            for gi, (window, dil) in enumerate(DILATED_GROUPS):
                o, lse = _dilated_group(b3(q), b3(kx), b3(vx), gi, dil=dil, max_dist=window // dil)
                outs.append(o)
                lses.append(lse)
            y = _dilated_merge(outs, lses, z)
        else:
            y = _moba_attention(b3(q), b3(kx), b3(vx), b3(z))
        x2 = _outproj(y.reshape(n_b * seq, D_MODEL), x2, w_out.astype(BF16),
                      ln_g.reshape(1, D_MODEL).astype(F32), ln_b.reshape(1, D_MODEL).astype(F32))
    return x2.reshape(n_b, seq, D_MODEL)
```

```python
import functools
import math

import numpy as np
import jax
import jax.numpy as jnp
from jax import lax
from jax.experimental import pallas as pl
from jax.experimental.pallas import tpu as pltpu

D_MODEL = 1024
HEAD_DIM = 64
N_HEADS = D_MODEL // HEAD_DIM
ROT_DIM = HEAD_DIM // 4
ROPE_THETA = 500000.0
DEPTH = 4
N_MIXERS = 3
KV_HEADS = (2, 4, 4)
WINDOW_A = 128
DILATED_GROUPS = ((128, 1), (512, 4), (2048, 16))
BAND = 128
MOBA_BLOCK = 256
MOBA_TOPK = 3
DEEPNORM_ALPHA = (2 * DEPTH) ** 0.25
LN_EPS = 1e-5
LOG2E = math.log2(math.e)
Q_SCALE = HEAD_DIM ** -0.5 * LOG2E

LANES = 128
LSE_LANES = LANES // N_HEADS
NEG_BIG = -1e30
VMEM_CAP = 60000 * 1024
TM = 512

F32 = jnp.float32
BF16 = jnp.bfloat16
NT_DIMS = (((1,), (1,)), ((), ()))


def _params(vmem_bytes, n_grid):
    limit = int(min(max(2 * vmem_bytes, 32 * 1024 * 1024), VMEM_CAP))
    return pltpu.CompilerParams(dimension_semantics=("arbitrary",) * n_grid,
                                vmem_limit_bytes=limit)


def _silu(z):
    return z / (1.0 + jnp.exp(-z))


def _fold_perm(n, dil):
    p = np.zeros((n, n), np.float32)
    idx = np.arange(n)
    p[(idx % dil) * (n // dil) + idx // dil, idx] = 1.0
    return p


def _inproj_kernel(*refs, dils, n_kvh, moba, seq_tiles):
    n_grp = len(dils)
    it = iter(refs)
    x_ref = next(it)
    groups = []
    for d in dils:
        perm_ref = next(it) if d > 1 else None
        groups.append((d, perm_ref, next(it), next(it), next(it), next(it)))
    wz_ref = next(it)
    outs = [(next(it), next(it), next(it)) for _ in range(n_grp)]
    z_ref = next(it)

    tm = x_ref.shape[0]
    xb = x_ref[...].astype(BF16)
    lane = lax.broadcasted_iota(jnp.int32, (tm, LANES), 1)
    lo = lane < HEAD_DIM
    if moba:
        row = lax.broadcasted_iota(jnp.int32, (tm, LANES), 0)
        pos = (pl.program_id(0) % seq_tiles) * tm + row
        k_aux = jnp.where(lane == HEAD_DIM + lax.shift_right_logical(pos, 8), 1.0, 0.0)

    for (d, perm_ref, rope_ref, wq_ref, wk_ref, wv_ref), (q_ref, kx_ref, vx_ref) in zip(groups, outs):
        n = tm // d
        xg = xb if d == 1 else jnp.dot(perm_ref[...], xb, preferred_element_type=F32).astype(BF16)
        cs, sa, sb = rope_ref[0], rope_ref[1], rope_ref[2]

        def rope(t):
            return (t * cs + pltpu.roll(t, LANES - ROT_DIM // 2, 1) * sa
                    + pltpu.roll(t, ROT_DIM // 2, 1) * sb)

        def put(ref, col, val):
            for r in range(d):
                ref[r, :, col:col + val.shape[1]] = val[r * n:(r + 1) * n]

        for c in range(D_MODEL // 512):
            acc = jnp.dot(xg, wq_ref[:, c * 512:(c + 1) * 512], preferred_element_type=F32)
            for s in range(4):
                put(q_ref, c * 512 + s * LANES, rope(acc[:, s * LANES:(s + 1) * LANES]).astype(BF16))
        acc_k = jnp.dot(xg, wk_ref[...], preferred_element_type=F32)
        acc_v = jnp.dot(xg, wv_ref[...], preferred_element_type=F32)
        for c in range(n_kvh // 2):
            k2 = rope(acc_k[:, c * LANES:(c + 1) * LANES])
            v2 = acc_v[:, c * LANES:(c + 1) * LANES]
            k2r = pltpu.roll(k2, HEAD_DIM, 1)
            v2r = pltpu.roll(v2, HEAD_DIM, 1)
            for e in range(2):
                j = 2 * c + e
                if moba:
                    kk = jnp.where(lo, k2 if e == 0 else k2r, k_aux)
                    vv = jnp.where(lo, v2 if e == 0 else v2r, 1.0)
                else:
                    kk = jnp.where(lo, k2, k2r) if e == 0 else jnp.where(lo, k2r, k2)
                    vv = jnp.where(lo, v2, v2r) if e == 0 else jnp.where(lo, v2r, v2)
                put(kx_ref, j * LANES, kk.astype(BF16))
                put(vx_ref, j * LANES, vv.astype(BF16))

    for c in range(D_MODEL // 512):
        acc = jnp.dot(xb, wz_ref[:, c * 512:(c + 1) * 512], preferred_element_type=F32)
        z_ref[:, c * 512:(c + 1) * 512] = acc.astype(BF16)


def _inproj(x2, rope_tabs, ws, wz, *, n_b, seq, dils, moba):
    tm = TM
    seq_tiles = seq // tm
    n_kvh = ws[0][1].shape[1] // HEAD_DIM
    kw = n_kvh * LANES
    const = lambda a: pl.BlockSpec(a.shape, lambda i: (0, 0), pipeline_mode=pl.Buffered(1))
    in_specs = [pl.BlockSpec((tm, D_MODEL), lambda i: (i, 0))]
    args = [x2]
    for d, tab, (wq, wk, wv) in zip(dils, rope_tabs, ws):
        if d > 1:
            perm = jnp.asarray(_fold_perm(tm, d), BF16)
            in_specs.append(const(perm))
            args.append(perm)
        in_specs += [pl.BlockSpec((3, tm, LANES), lambda i: (0, i % seq_tiles, 0)),
                     const(wq), const(wk), const(wv)]
        args += [tab, wq, wk, wv]
    in_specs.append(const(wz))
    args.append(wz)
    out_specs, out_shape = [], []
    for d in dils:
        omap = lambda i: (i // seq_tiles, 0, i % seq_tiles, 0)
        for w in (D_MODEL, kw, kw):
            out_specs.append(pl.BlockSpec((None, d, tm // d, w), omap))
            out_shape.append(jax.ShapeDtypeStruct((n_b, d, seq // d, w), BF16))
    out_specs.append(pl.BlockSpec((tm, D_MODEL), lambda i: (i, 0)))
    out_shape.append(jax.ShapeDtypeStruct((n_b * seq, D_MODEL), BF16))
    n_w = sum(wq.shape[1] + wk.shape[1] + wv.shape[1] for wq, wk, wv in ws) + D_MODEL
    vmem = (2 * tm * D_MODEL * 4 + len(dils) * (2 * 3 * tm * LANES * 4 + tm * tm * 2)
            + 2 * D_MODEL * n_w + 2 * 2 * tm * (len(dils) * (D_MODEL + 2 * kw) + D_MODEL)
            + 6 * tm * 512 * 4)
    res = pl.pallas_call(
        functools.partial(_inproj_kernel, dils=tuple(dils), n_kvh=n_kvh, moba=moba, seq_tiles=seq_tiles),
        grid=(n_b * seq // tm,),
        in_specs=in_specs, out_specs=out_specs, out_shape=out_shape,
        compiler_params=_params(vmem, 1),
        name="inproj",
    )(*args)
    return [tuple(res[3 * g:3 * g + 3]) for g in range(len(dils))], res[-1]


def _outproj_kernel(y_ref, x_ref, w_ref, g_ref, b_ref, o_ref):
    t = DEEPNORM_ALPHA * x_ref[...] + jnp.dot(y_ref[...], w_ref[...], preferred_element_type=F32)
    mu = jnp.mean(t, axis=-1, keepdims=True)
    d = t - mu
    var = jnp.mean(d * d, axis=-1, keepdims=True)
    o_ref[...] = d * lax.rsqrt(var + LN_EPS) * g_ref[...] + b_ref[...]


def _outproj(y2, x2, w, g, b):
    t_rows = x2.shape[0]
    tm = TM
    row = lambda i: (i, 0)
    full = lambda i: (0, 0)
    vmem = 2 * tm * D_MODEL * (2 + 4 + 4) + 2 * D_MODEL * D_MODEL * 2 + 3 * tm * D_MODEL * 4
    return pl.pallas_call(
        _outproj_kernel,
        grid=(t_rows // tm,),
        in_specs=[pl.BlockSpec((tm, D_MODEL), row), pl.BlockSpec((tm, D_MODEL), row),
                  pl.BlockSpec((D_MODEL, D_MODEL), full),
                  pl.BlockSpec((1, D_MODEL), full), pl.BlockSpec((1, D_MODEL), full)],
        out_specs=pl.BlockSpec((tm, D_MODEL), row),
        out_shape=jax.ShapeDtypeStruct((t_rows, D_MODEL), F32),
        compiler_params=_params(vmem, 1),
        name="outproj_ln",
    )(y2, x2, w, g, b)


def _band_consts(max_dist, sink):
    i = np.arange(BAND)[:, None]
    c = np.arange(2 * BAND)[None, :]
    dist = BAND + i - c
    ok = (dist >= 0) & (dist <= max_dist)
    first = ok & (c >= BAND)
    sel = np.zeros((2, 2 * BAND, LANES), np.float32)
    sel[0, :, :HEAD_DIM] = 1.0
    sel[1, :, HEAD_DIM:] = 1.0
    ones = sel.copy()
    if sink is not None:
        assert not ok[:, 0].any()
        ok[:, 0] = True
        first[:, 0] = True
        sel[:, 0, :] = 0.0
    mask = np.where(np.stack([first, ok]), 0.0, NEG_BIG)
    consts = [jnp.asarray(np.concatenate([mask, mask], axis=2), F32),
              jnp.asarray(sel, BF16), jnp.asarray(ones, BF16)]
    if sink is not None:
        s2 = sink.astype(F32) * LOG2E
        hi = s2.astype(BF16)
        lo = (s2 - hi.astype(F32)).astype(BF16)
        per_pair = jnp.stack([hi[0::2], lo[0::2], hi[1::2], lo[1::2]], axis=1)
        per_pair = jnp.pad(per_pair, ((0, 0), (0, LANES - 4)))
        k_aux = np.zeros((2, 2 * BAND, LANES), np.float32)
        k_aux[0, 0, 0:2] = 1.0
        k_aux[1, 0, 2:4] = 1.0
        consts += [jnp.repeat(per_pair, BAND, axis=0), jnp.asarray(k_aux, BF16)]
    return consts


def _band_kernel(*refs, n_kvh, has_sink, tile_axis):
    if has_sink:
        (mask_ref, sel_ref, ones_ref, qaux_ref, kaux_ref, q_ref, kc_ref, kp_ref, vc_ref, vp_ref, z_ref,
         y_ref, kbuf, vbuf) = refs
    else:
        mask_ref, sel_ref, ones_ref, q_ref, kc_ref, kp_ref, vc_ref, vp_ref, o_ref, lse_ref, kbuf, vbuf = refs
    n_pair = N_HEADS // n_kvh // 2
    tq = q_ref.shape[0]
    i_tile = pl.program_id(tile_axis)
    kbuf[0:BAND, :] = kp_ref[...]
    kbuf[BAND:BAND + tq, :] = kc_ref[...]
    vbuf[0:BAND, :] = vp_ref[...]
    vbuf[BAND:BAND + tq, :] = vc_ref[...]
    lane = lax.broadcasted_iota(jnp.int32, (BAND, LANES), 1)
    lo = lane < HEAD_DIM

    def block(bb, carry):
        r0 = pl.multiple_of(bb * BAND, BAND)
        mask = mask_ref[jnp.where((i_tile == 0) & (bb == 0), 0, 1)]
        lse_acc = jnp.zeros((BAND, LANES), F32)
        for j in range(n_kvh):
            kcat = kbuf[pl.ds(r0, 2 * BAND), j * LANES:(j + 1) * LANES]
            vcat = vbuf[pl.ds(r0, 2 * BAND), j * LANES:(j + 1) * LANES]
            k_e, k_o = kcat * sel_ref[0], kcat * sel_ref[1]
            v_e = jnp.concatenate([vcat * sel_ref[0], ones_ref[0]], axis=1)
            v_o = jnp.concatenate([vcat * sel_ref[1], ones_ref[1]], axis=1)
            qp = jnp.concatenate(
                [q_ref[pl.ds(r0, BAND), (j * n_pair + pi) * LANES:(j * n_pair + pi + 1) * LANES]
                 for pi in range(n_pair)], axis=0)
            if has_sink:
                qp = jnp.concatenate([qp, qaux_ref[j * n_pair * BAND:(j + 1) * n_pair * BAND, :]], axis=1)
                k_e = jnp.concatenate([k_e, kaux_ref[0]], axis=1)
                k_o = jnp.concatenate([k_o, kaux_ref[1]], axis=1)
            kbd = jnp.concatenate([k_e, k_o], axis=0)
            vbd = jnp.concatenate([v_e, v_o], axis=0)
            s = lax.dot_general(qp, kbd, NT_DIMS, preferred_element_type=F32)
            ps, ms = [], []
            for pi in range(n_pair):
                sp = s[pi * BAND:(pi + 1) * BAND] + mask
                m_e = jnp.max(sp[:, 0:2 * BAND], axis=-1, keepdims=True)
                m_o = jnp.max(sp[:, 2 * BAND:], axis=-1, keepdims=True)
                m_full = jnp.concatenate([jnp.broadcast_to(m_e, (BAND, 2 * BAND)),
                                          jnp.broadcast_to(m_o, (BAND, 2 * BAND))], axis=1)
                ms.append((m_e, m_o))
                ps.append(jnp.exp2(sp - m_full).astype(BF16))
            p = jnp.concatenate(ps, axis=0)
            oext = jnp.dot(p, vbd, preferred_element_type=F32)
            for pi in range(n_pair):
                oe = oext[pi * BAND:(pi + 1) * BAND]
                o_slab = oe[:, 0:LANES] / oe[:, LANES:]
                pair = j * n_pair + pi
                col = pair * LANES
                if has_sink:
                    zf = z_ref[pl.ds(r0, BAND), col:col + LANES].astype(F32)
                    y_ref[pl.ds(r0, BAND), col:col + LANES] = (o_slab * _silu(zf)).astype(BF16)
                else:
                    o_ref[pl.ds(r0, BAND), col:col + LANES] = o_slab.astype(BF16)
                    lse_slab = jnp.where(lo, ms[pi][0], ms[pi][1]) + jnp.log2(oe[:, LANES:])
                    a, b = pair * LSE_LANES, HEAD_DIM + pair * LSE_LANES
                    mine = ((lane >= a) & (lane < a + LSE_LANES)) | ((lane >= b) & (lane < b + LSE_LANES))
                    lse_acc = jnp.where(mine, lse_slab, lse_acc)
        if not has_sink:
            lse_ref[pl.ds(r0, BAND), :] = lse_acc
        return carry

    lax.fori_loop(0, tq // BAND, block, 0)


def _swa_attention(q, kx, vx, z, sink, *, max_dist):
    n_b, seq, _ = q.shape
    kw = kx.shape[2]
    n_kvh = kw // LANES
    tq = 512
    bpt = tq // BAND
    cur = lambda b, i: (b, i, 0)
    prev = lambda b, i: (b, jnp.maximum(i * bpt - 1, 0), 0)
    consts = _band_consts(max_dist, sink)
    const_specs = [pl.BlockSpec(c.shape, lambda b, i, n=c.ndim: (0,) * n) for c in consts]
    vmem = (2 * tq * D_MODEL * 2 * 3 + 6 * (tq + BAND) * kw * 2 * 2
            + 2 * sum(int(np.prod(c.shape)) * c.dtype.itemsize for c in consts) + 16 * 4 * BAND * 4 * BAND * 4)
    return pl.pallas_call(
        functools.partial(_band_kernel, n_kvh=n_kvh, has_sink=True, tile_axis=1),
        grid=(n_b, seq // tq),
        in_specs=const_specs + [
            pl.BlockSpec((None, tq, D_MODEL), cur),
            pl.BlockSpec((None, tq, kw), cur), pl.BlockSpec((None, BAND, kw), prev),
            pl.BlockSpec((None, tq, kw), cur), pl.BlockSpec((None, BAND, kw), prev),
            pl.BlockSpec((None, tq, D_MODEL), cur)],
        out_specs=pl.BlockSpec((None, tq, D_MODEL), cur),
        out_shape=jax.ShapeDtypeStruct((n_b, seq, D_MODEL), BF16),
        scratch_shapes=[pltpu.VMEM((tq + BAND, kw), BF16), pltpu.VMEM((tq + BAND, kw), BF16)],
        compiler_params=_params(vmem, 2),
        name="swa_attention",
    )(*consts, q, kx, kx, vx, vx, z)


def _dilated_group(q, kx, vx, gi, *, max_dist):
    n_b, dil, fold, _ = q.shape
    kw = kx.shape[3]
    n_kvh = kw // LANES
    tq = min(512, fold)
    bpt = tq // BAND
    cur = lambda b, r, i: (b, r, i, 0)
    prev = lambda b, r, i: (b, r, jnp.maximum(i * bpt - 1, 0), 0)
    consts = _band_consts(max_dist, None)
    const_specs = [pl.BlockSpec(c.shape, lambda b, r, i, n=c.ndim: (0,) * n) for c in consts]
    vmem = (2 * tq * D_MODEL * 2 * 2 + 2 * tq * LANES * 4 + 6 * (tq + BAND) * kw * 2 * 2
            + 2 * sum(int(np.prod(c.shape)) * c.dtype.itemsize for c in consts) + 16 * 2 * BAND * 4 * BAND * 4)
    return pl.pallas_call(
        functools.partial(_band_kernel, n_kvh=n_kvh, has_sink=False, tile_axis=2),
        grid=(n_b, dil, fold // tq),
        in_specs=const_specs + [
            pl.BlockSpec((None, None, tq, D_MODEL), cur),
            pl.BlockSpec((None, None, tq, kw), cur), pl.BlockSpec((None, None, BAND, kw), prev),
            pl.BlockSpec((None, None, tq, kw), cur), pl.BlockSpec((None, None, BAND, kw), prev)],
        out_specs=[pl.BlockSpec((None, None, tq, D_MODEL), cur), pl.BlockSpec((None, None, tq, LANES), cur)],
        out_shape=[jax.ShapeDtypeStruct((n_b, dil, fold, D_MODEL), BF16),
                   jax.ShapeDtypeStruct((n_b, dil, fold, LANES), F32)],
        scratch_shapes=[pltpu.VMEM((tq + BAND, kw), BF16), pltpu.VMEM((tq + BAND, kw), BF16)],
        compiler_params=_params(vmem, 3),
        name="dilated_group_%d" % gi,
    )(*consts, q, kx, kx, vx, vx)


def _dilated_merge_kernel(*refs, dils):
    n_grp = len(dils)
    it = iter(refs)
    o_refs = [next(it) for _ in range(n_grp)]
    l_refs = [next(it) for _ in range(n_grp)]
    p_refs = [next(it) if d > 1 else None for d in dils]
    e_ref, z_ref, y_ref = next(it), next(it), next(it)
    tm = z_ref.shape[0]

    def split3(a):
        hi = a.astype(BF16)
        r1 = a - hi.astype(F32)
        mid = r1.astype(BF16)
        return hi, mid, (r1 - mid.astype(F32)).astype(BF16)

    outs, lses = [], []
    for d, o_ref, l_ref, p_ref in zip(dils, o_refs, l_refs, p_refs):
        o = o_ref[...].reshape(tm, D_MODEL)
        lse = l_ref[...].reshape(tm, LANES)
        if d > 1:
            pt = p_ref[...]
            o = jnp.dot(pt, o, preferred_element_type=F32)
            parts = jnp.concatenate(split3(lse), axis=1)
            lp = jnp.dot(pt, parts, preferred_element_type=F32)
            lse = lp[:, 0:LANES] + lp[:, LANES:2 * LANES] + lp[:, 2 * LANES:]
        else:
            o = o.astype(F32)
        outs.append(o)
        lses.append(lse)
    m = functools.reduce(jnp.maximum, lses)
    es = [jnp.exp2(l - m) for l in lses]
    den = functools.reduce(lambda a, b: a + b, es)
    acc = jnp.zeros((tm, D_MODEL), F32)
    for e, o in zip(es, outs):
        hi, mid, _ = split3(e / den)
        w = jnp.dot(jnp.concatenate([hi, mid], axis=1), e_ref[...], preferred_element_type=F32)
        acc = acc + w * o
    y_ref[...] = (acc * _silu(z_ref[...].astype(F32))).astype(BF16)


def _dilated_merge(outs, lses, z2, *, n_b, seq, dils):
    tm = 256
    seq_tiles = seq // tm
    expand = np.zeros((LANES, D_MODEL), np.float32)
    for h in range(N_HEADS):
        expand[(h % 2) * HEAD_DIM + (h // 2) * LSE_LANES, h * HEAD_DIM:(h + 1) * HEAD_DIM] = 1.0
    expand = jnp.asarray(np.concatenate([expand, expand], axis=0), BF16)
    omap = lambda i: (i // seq_tiles, 0, i % seq_tiles, 0)
    in_specs = [pl.BlockSpec((None, d, tm // d, D_MODEL), omap) for d in dils]
    in_specs += [pl.BlockSpec((None, d, tm // d, LANES), omap) for d in dils]
    perms = [jnp.asarray(_fold_perm(tm, d).T, BF16) for d in dils if d > 1]
    in_specs += [pl.BlockSpec((tm, tm), lambda i: (0, 0)) for _ in perms]
    in_specs += [pl.BlockSpec(expand.shape, lambda i: (0, 0)), pl.BlockSpec((tm, D_MODEL), lambda i: (i, 0))]
    vmem = (2 * tm * D_MODEL * (3 * 2 + 2 + 2) + 2 * 3 * tm * LANES * 4 + 4 * tm * tm * 2
            + 2 * 2 * LANES * D_MODEL * 2 + 10 * tm * D_MODEL * 4)
    return pl.pallas_call(
        functools.partial(_dilated_merge_kernel, dils=tuple(dils)),
        grid=(n_b * seq // tm,),
        in_specs=in_specs,
        out_specs=pl.BlockSpec((tm, D_MODEL), lambda i: (i, 0)),
        out_shape=jax.ShapeDtypeStruct((n_b * seq, D_MODEL), BF16),
        compiler_params=_params(vmem, 1),
        name="dilated_merge",
    )(*outs, *lses, *perms, expand, z2)


def _moba_kernel(tril_ref, q_ref, kx_ref, vx_ref, z_ref, y_ref,
                 km_scr, qext_scr, sa_scr, sb_scr, m_scr, acc_scr, *, n_blk):
    t = pl.program_id(2)
    g = 4
    tq = MOBA_BLOCK
    rows_n = g * tq
    lane = lax.broadcasted_iota(jnp.int32, (tq, LANES), 1)
    lo = lane < HEAD_DIM

    @pl.when(t == 0)
    def _block_means():
        means = []
        for n in range(n_blk):
            kb = kx_ref[n * MOBA_BLOCK:(n + 1) * MOBA_BLOCK, :].astype(F32)
            means.append(jnp.sum(kb, axis=0, keepdims=True) * (1.0 / MOBA_BLOCK))
        km = jnp.concatenate(means, axis=0)
        km = jnp.where(lax.broadcasted_iota(jnp.int32, km.shape, 1) < HEAD_DIM, km, 0.0)
        km_scr[...] = jnp.concatenate(
            [jnp.zeros((HEAD_DIM, LANES), F32), km,
             jnp.zeros((LANES - HEAD_DIM - n_blk, LANES), F32)], axis=0).astype(BF16)

    rows = []
    for gi in range(g):
        slab = q_ref[:, (gi // 2) * LANES:(gi // 2 + 1) * LANES].astype(F32)
        if gi % 2:
            slab = pltpu.roll(slab, HEAD_DIM, 1)
        rows.append(jnp.where(lo, slab, 0.0))
    qrows = jnp.concatenate(rows, axis=0)

    gate_t = lax.dot_general(km_scr[...], qrows.astype(BF16), NT_DIMS, preferred_element_type=F32)
    gate = gate_t[HEAD_DIM:HEAD_DIM + n_blk]
    blk = lax.broadcasted_iota(jnp.int32, gate.shape, 0).astype(F32)
    tf = t.astype(F32)
    gate = jnp.where(blk < tf, gate, -jnp.inf)
    sel = jnp.zeros(gate.shape, F32)
    for _ in range(MOBA_TOPK):
        mx = jnp.max(gate, axis=0, keepdims=True)
        is_max = (gate == mx) & (mx > -jnp.inf)
        first = jnp.min(jnp.where(is_max, blk, float(n_blk)), axis=0, keepdims=True)
        pick = blk == first
        sel = jnp.where(pick, 1.0, sel)
        gate = jnp.where(pick, -jnp.inf, gate)
    bias_t = jnp.where((sel > 0.0) | (blk == tf), 0.0, NEG_BIG)
    bias_full = jnp.concatenate(
        [jnp.zeros((HEAD_DIM, rows_n), F32), bias_t,
         jnp.zeros((LANES - HEAD_DIM - n_blk, rows_n), F32)], axis=0)
    lane_r = lax.broadcasted_iota(jnp.int32, (rows_n, LANES), 1)
    qext = jnp.where(lane_r < HEAD_DIM, qrows, bias_full.T).astype(BF16)
    qext_scr[...] = qext

    def scores(blk_idx):
        r0 = pl.multiple_of(blk_idx * MOBA_BLOCK, MOBA_BLOCK)
        return lax.dot_general(qext_scr[...], kx_ref[pl.ds(r0, MOBA_BLOCK), :], NT_DIMS,
                               preferred_element_type=F32)

    def consume(sn, blk_idx):
        r0 = pl.multiple_of(blk_idx * MOBA_BLOCK, MOBA_BLOCK)
        m_old = m_scr[...]
        m_new = jnp.maximum(m_old, jnp.max(sn, axis=-1, keepdims=True))
        alpha = jnp.exp2(m_old - m_new)
        pn = jnp.exp2(sn - jnp.concatenate([m_new, m_new], axis=1)).astype(BF16)
        acc_scr[...] = alpha * acc_scr[...] + jnp.dot(pn, vx_ref[pl.ds(r0, MOBA_BLOCK), :],
                                                      preferred_element_type=F32)
        m_scr[...] = m_new

    m_scr[...] = jnp.full((rows_n, LANES), NEG_BIG, F32)
    acc_scr[...] = jnp.zeros((rows_n, LANES), F32)
    s0 = scores(t)
    tril = tril_ref[...]
    sa_scr[...] = jnp.concatenate([s0[gi * tq:(gi + 1) * tq] + tril for gi in range(g)], axis=0)
    last_past = jnp.maximum(t - 1, 0)

    def pair(kk, carry):
        k0 = 2 * kk
        sb_scr[...] = scores(jnp.minimum(k0, last_past))
        consume(sa_scr[...], jnp.where(k0 == 0, t, k0 - 1))
        sa_scr[...] = scores(jnp.minimum(k0 + 1, last_past))
        consume(sb_scr[...] + jnp.where(k0 < t, 0.0, NEG_BIG), jnp.minimum(k0, last_past))
        return carry

    lax.fori_loop(0, t // 2 + 1, pair, 0)

    acc = acc_scr[...]
    for pi in range(g // 2):
        oe = acc[(2 * pi) * tq:(2 * pi + 1) * tq]
        oo = acc[(2 * pi + 1) * tq:(2 * pi + 2) * tq]
        o_slab = jnp.where(lo, oe / pltpu.roll(oe, HEAD_DIM, 1), pltpu.roll(oo, HEAD_DIM, 1) / oo)
        zf = z_ref[:, pi * LANES:(pi + 1) * LANES].astype(F32)
        y_ref[:, pi * LANES:(pi + 1) * LANES] = (o_slab * _silu(zf)).astype(BF16)


def _moba_attention(q, kx, vx, z):
    n_b, seq, _ = q.shape
    n_kvh = kx.shape[2] // LANES
    n_blk = seq // MOBA_BLOCK
    g = N_HEADS // n_kvh
    tq = MOBA_BLOCK
    qw = g * HEAD_DIM
    tril = jnp.asarray(np.where(np.tril(np.ones((tq, tq), bool)), 0.0, NEG_BIG), F32)
    tile = lambda b, j, t: (b, t, j)
    whole = lambda b, j, t: (b, 0, j)
    vmem = (2 * 2 * seq * LANES * 2 + 2 * 3 * tq * qw * 2 + tq * tq * 4 * 2
            + g * tq * LANES * (2 + 4 + 4) + 2 * g * tq * tq * 4 + 6 * g * tq * tq * 4)
    return pl.pallas_call(
        functools.partial(_moba_kernel, n_blk=n_blk),
        grid=(n_b, n_kvh, seq // tq),
        in_specs=[pl.BlockSpec((tq, tq), lambda b, j, t: (0, 0)),
                  pl.BlockSpec((None, tq, qw), tile),
                  pl.BlockSpec((None, seq, LANES), whole), pl.BlockSpec((None, seq, LANES), whole),
                  pl.BlockSpec((None, tq, qw), tile)],
        out_specs=pl.BlockSpec((None, tq, qw), tile),
        out_shape=jax.ShapeDtypeStruct((n_b, seq, D_MODEL), BF16),
        scratch_shapes=[pltpu.VMEM((LANES, LANES), BF16), pltpu.VMEM((g * tq, LANES), BF16),
                        pltpu.VMEM((g * tq, tq), F32), pltpu.VMEM((g * tq, tq), F32),
                        pltpu.VMEM((g * tq, LANES), F32), pltpu.VMEM((g * tq, LANES), F32)],
        compiler_params=_params(vmem, 3),
        name="moba_attention",
    )(tril, q, kx, vx, z)


def _rope_table(seq, dil):
    half = ROT_DIM // 2
    inv = ROPE_THETA ** (-jnp.arange(0, ROT_DIM, 2, dtype=F32) / ROT_DIM)
    ang = jnp.arange(seq, dtype=F32)[:, None] * inv[None, :]
    cos, sin = jnp.cos(ang), jnp.sin(ang)
    rest = HEAD_DIM - ROT_DIM
    zero_h = jnp.zeros((seq, half), F32)
    c64 = jnp.concatenate([cos, cos, jnp.ones((seq, rest), F32)], axis=1)
    a64 = jnp.concatenate([-sin, zero_h, jnp.zeros((seq, rest), F32)], axis=1)
    b64 = jnp.concatenate([zero_h, sin, jnp.zeros((seq, rest), F32)], axis=1)
    tab = jnp.stack([jnp.concatenate([t, t], axis=1) for t in (c64, a64, b64)])
    if dil > 1:
        tab = tab.reshape(3, seq // TM, TM // dil, dil, LANES).transpose(0, 1, 3, 2, 4).reshape(3, seq, LANES)
    return tab


def _split_w_in(w_in, kind):
    kv = KV_HEADS[kind] * HEAD_DIM
    n_grp = len(DILATED_GROUPS) if kind == 1 else 1
    ws = []
    off = 0
    for _ in range(n_grp):
        wq = w_in[:, off:off + D_MODEL]; off += D_MODEL
        wk = w_in[:, off:off + kv]; off += kv
        wv = w_in[:, off:off + kv]; off += kv
        ws.append(((wq * Q_SCALE).astype(BF16), wk.astype(BF16), wv.astype(BF16)))
    return ws, w_in[:, off:off + D_MODEL].astype(BF16)


def kernel(x, w_in_0, sink_0, w_out_0, ln_g_0, ln_b_0, w_in_1, w_out_1, ln_g_1, ln_b_1, w_in_2, w_out_2, ln_g_2, ln_b_2, w_in_3, sink_3, w_out_3, ln_g_3, ln_b_3):
    n_b, seq, d_model = x.shape
    assert d_model == D_MODEL and seq % max(dl * BAND for _, dl in DILATED_GROUPS) == 0 and seq % TM == 0
    assert (seq // MOBA_BLOCK) % 8 == 0 and seq // MOBA_BLOCK <= LANES - HEAD_DIM
    layers = [(w_in_0, sink_0, w_out_0, ln_g_0, ln_b_0),
              (w_in_1, None, w_out_1, ln_g_1, ln_b_1),
              (w_in_2, None, w_out_2, ln_g_2, ln_b_2),
              (w_in_3, sink_3, w_out_3, ln_g_3, ln_b_3)]
    dil_b = [dl for _, dl in DILATED_GROUPS]
    rope_nat = _rope_table(seq, 1)
    x2 = x.reshape(n_b * seq, D_MODEL)
    for i, (w_in, sink, w_out, ln_g, ln_b) in enumerate(layers):
        kind = i % N_MIXERS
        ws, wz = _split_w_in(w_in, kind)
        nat = lambda a: a.reshape(n_b, seq, a.shape[-1])
        if kind == 1:
            tabs = [_rope_table(seq, dl) for dl in dil_b]
            grp, z = _inproj(x2, tabs, ws, wz, n_b=n_b, seq=seq, dils=dil_b, moba=False)
            outs, lses = [], []
            for gi, ((window, dl), (q, kx, vx)) in enumerate(zip(DILATED_GROUPS, grp)):
                o, lse = _dilated_group(q, kx, vx, gi, max_dist=window // dl)
                outs.append(o)
                lses.append(lse)
            y = _dilated_merge(outs, lses, z, n_b=n_b, seq=seq, dils=dil_b)
        else:
            grp, z = _inproj(x2, [rope_nat], ws, wz, n_b=n_b, seq=seq, dils=[1], moba=(kind == 2))
            q, kx, vx = (nat(a) for a in grp[0])
            if kind == 0:
                y = _swa_attention(q, kx, vx, nat(z), sink, max_dist=WINDOW_A - 1)
            else:
                y = _moba_attention(q, kx, vx, nat(z))
        x2 = _outproj(y.reshape(n_b * seq, D_MODEL), x2, w_out.astype(BF16),
                      ln_g.reshape(1, D_MODEL).astype(F32), ln_b.reshape(1, D_MODEL).astype(F32))
    return x2.reshape(n_b, seq, D_MODEL)
```

```python
import functools
import math

import numpy as np
import jax
import jax.numpy as jnp
from jax import lax
from jax.experimental import pallas as pl
from jax.experimental.pallas import tpu as pltpu

D_MODEL = 1024
HEAD_DIM = 64
N_HEADS = D_MODEL // HEAD_DIM
ROT_DIM = HEAD_DIM // 4
ROPE_THETA = 500000.0
DEPTH = 4
N_MIXERS = 3
KV_HEADS = (2, 4, 4)
WINDOW_A = 128
DILATED_GROUPS = ((128, 1), (512, 4), (2048, 16))
BAND = 128
MOBA_BLOCK = 256
MOBA_TOPK = 3
DEEPNORM_ALPHA = (2 * DEPTH) ** 0.25
LN_EPS = 1e-5
LOG2E = math.log2(math.e)
Q_SCALE = HEAD_DIM ** -0.5 * LOG2E

LANES = 128
LSE_LANES = LANES // N_HEADS
NEG_BIG = -1e30
VMEM_CAP = 60000 * 1024
TM = 512
MERGE_CHUNK = 256

F32 = jnp.float32
BF16 = jnp.bfloat16
NT_DIMS = (((1,), (1,)), ((), ()))


def _params(vmem_bytes, n_grid):
    limit = int(min(max(2 * vmem_bytes, 32 * 1024 * 1024), VMEM_CAP))
    return pltpu.CompilerParams(dimension_semantics=("arbitrary",) * n_grid,
                                vmem_limit_bytes=limit)


def _silu(z):
    return z / (1.0 + jnp.exp(-z))


def _fold_perm(n, dil):
    p = np.zeros((n, n), np.float32)
    idx = np.arange(n)
    p[(idx % dil) * (n // dil) + idx // dil, idx] = 1.0
    return p


def _inproj_kernel(*refs, dils, n_kvh, moba, seq_tiles):
    n_grp = len(dils)
    kv = n_kvh * HEAD_DIM
    it = iter(refs)
    x_ref = next(it)
    groups = [(d, next(it) if d > 1 else None, next(it)) for d in dils]
    w_ref = next(it)
    outs = [(next(it), next(it), next(it)) for _ in range(n_grp)]
    z_ref = next(it)

    def proj(lhs, col, width):
        return jnp.dot(lhs, w_ref[:, col:col + width], preferred_element_type=F32)

    tm = x_ref.shape[0]
    xb = x_ref[...].astype(BF16)
    lane = lax.broadcasted_iota(jnp.int32, (tm, LANES), 1)
    lo = lane < HEAD_DIM
    if moba:
        row = lax.broadcasted_iota(jnp.int32, (tm, LANES), 0)
        pos = (pl.program_id(0) % seq_tiles) * tm + row
        k_aux = jnp.where(lane == HEAD_DIM + lax.shift_right_logical(pos, 8), 1.0, 0.0)

    off = 0
    for (d, perm_ref, rope_ref), (q_ref, kx_ref, vx_ref) in zip(groups, outs):
        n = tm // d
        xg = xb if d == 1 else jnp.dot(perm_ref[...], xb, preferred_element_type=F32).astype(BF16)
        cs, sa, sb = rope_ref[0], rope_ref[1], rope_ref[2]

        def rope(t):
            return (t * cs + pltpu.roll(t, LANES - ROT_DIM // 2, 1) * sa
                    + pltpu.roll(t, ROT_DIM // 2, 1) * sb)

        def put(ref, col, val):
            for r in range(d):
                ref[r, :, col:col + val.shape[1]] = val[r * n:(r + 1) * n]

        for c in range(D_MODEL // 512):
            acc = proj(xg, off + c * 512, 512)
            for s in range(4):
                put(q_ref, c * 512 + s * LANES, rope(acc[:, s * LANES:(s + 1) * LANES]).astype(BF16))
        acc_k = proj(xg, off + D_MODEL, kv)
        acc_v = proj(xg, off + D_MODEL + kv, kv)
        off += D_MODEL + 2 * kv
        for c in range(n_kvh // 2):
            k2 = rope(acc_k[:, c * LANES:(c + 1) * LANES])
            v2 = acc_v[:, c * LANES:(c + 1) * LANES]
            k2r = pltpu.roll(k2, HEAD_DIM, 1)
            v2r = pltpu.roll(v2, HEAD_DIM, 1)
            for e in range(2):
                j = 2 * c + e
                if moba:
                    kk = jnp.where(lo, k2 if e == 0 else k2r, k_aux)
                    vv = jnp.where(lo, v2 if e == 0 else v2r, 1.0)
                else:
                    kk = jnp.where(lo, k2, k2r) if e == 0 else jnp.where(lo, k2r, k2)
                    vv = jnp.where(lo, v2, v2r) if e == 0 else jnp.where(lo, v2r, v2)
                put(kx_ref, j * LANES, kk.astype(BF16))
                put(vx_ref, j * LANES, vv.astype(BF16))

    for c in range(D_MODEL // 512):
        z_ref[:, c * 512:(c + 1) * 512] = proj(xb, off + c * 512, 512).astype(BF16)


def _inproj(x2, rope_tabs, w, *, n_b, seq, dils, n_kvh, moba):
    tm = TM
    seq_tiles = seq // tm
    kw = n_kvh * LANES
    const = lambda a: pl.BlockSpec(a.shape, lambda i: (0, 0), pipeline_mode=pl.Buffered(1))
    in_specs = [pl.BlockSpec((tm, D_MODEL), lambda i: (i, 0))]
    args = [x2]
    for d, tab in zip(dils, rope_tabs):
        if d > 1:
            perm = jnp.asarray(_fold_perm(tm, d), BF16)
            in_specs.append(const(perm))
            args.append(perm)
        in_specs.append(pl.BlockSpec((3, tm, LANES), lambda i: (0, i % seq_tiles, 0)))
        args.append(tab)
    in_specs.append(const(w))
    args.append(w)
    out_specs, out_shape = [], []
    for d in dils:
        omap = lambda i: (i // seq_tiles, 0, i % seq_tiles, 0)
        for width in (D_MODEL, kw, kw):
            out_specs.append(pl.BlockSpec((None, d, tm // d, width), omap))
            out_shape.append(jax.ShapeDtypeStruct((n_b, d, seq // d, width), BF16))
    out_specs.append(pl.BlockSpec((tm, D_MODEL), lambda i: (i, 0)))
    out_shape.append(jax.ShapeDtypeStruct((n_b * seq, D_MODEL), BF16))
    vmem = (2 * tm * D_MODEL * 4 + len(dils) * (2 * 3 * tm * LANES * 4 + tm * tm * 2)
            + 2 * D_MODEL * w.shape[1] + 2 * 2 * tm * (len(dils) * (D_MODEL + 2 * kw) + D_MODEL)
            + 6 * tm * 512 * 4)
    res = pl.pallas_call(
        functools.partial(_inproj_kernel, dils=tuple(dils), n_kvh=n_kvh, moba=moba, seq_tiles=seq_tiles),
        grid=(n_b * seq // tm,),
        in_specs=in_specs, out_specs=out_specs, out_shape=out_shape,
        compiler_params=_params(vmem, 1),
        name="inproj",
    )(*args)
    return [tuple(res[3 * g:3 * g + 3]) for g in range(len(dils))], res[-1]


def _outproj_kernel(y_ref, x_ref, w_ref, g_ref, b_ref, o_ref):
    rows = 128
    for c in range(y_ref.shape[0] // rows):
        sl = slice(c * rows, (c + 1) * rows)
        t = DEEPNORM_ALPHA * x_ref[sl, :] + jnp.dot(y_ref[sl, :], w_ref[...], preferred_element_type=F32)
        mu = jnp.mean(t, axis=-1, keepdims=True)
        d = t - mu
        var = jnp.mean(d * d, axis=-1, keepdims=True)
        o_ref[sl, :] = d * lax.rsqrt(var + LN_EPS) * g_ref[...] + b_ref[...]


def _outproj(y2, x2, w, g, b):
    t_rows = x2.shape[0]
    tm = 2 * TM
    row = lambda i: (i, 0)
    full = lambda i: (0, 0)
    vmem = 2 * tm * D_MODEL * (2 + 4 + 4) + 2 * D_MODEL * D_MODEL * 2 + 3 * 128 * D_MODEL * 4
    return pl.pallas_call(
        _outproj_kernel,
        grid=(t_rows // tm,),
        in_specs=[pl.BlockSpec((tm, D_MODEL), row), pl.BlockSpec((tm, D_MODEL), row),
                  pl.BlockSpec((D_MODEL, D_MODEL), full),
                  pl.BlockSpec((1, D_MODEL), full), pl.BlockSpec((1, D_MODEL), full)],
        out_specs=pl.BlockSpec((tm, D_MODEL), row),
        out_shape=jax.ShapeDtypeStruct((t_rows, D_MODEL), F32),
        compiler_params=_params(vmem, 1),
        name="outproj_ln",
    )(y2, x2, w, g, b)


def _band_consts(max_dist, sink):
    i = np.arange(BAND)[:, None]
    c = np.arange(2 * BAND)[None, :]
    dist = BAND + i - c
    ok = (dist >= 0) & (dist <= max_dist)
    first = ok & (c >= BAND)
    sel = np.zeros((2, 2 * BAND, LANES), np.float32)
    sel[0, :, :HEAD_DIM] = 1.0
    sel[1, :, HEAD_DIM:] = 1.0
    ones = sel.copy()
    if sink is not None:
        assert not ok[:, 0].any()
        ok[:, 0] = True
        first[:, 0] = True
        sel[:, 0, :] = 0.0
    mask = np.where(np.stack([first, ok]), 0.0, NEG_BIG)
    consts = [jnp.asarray(np.concatenate([mask, mask], axis=2), F32),
              jnp.asarray(sel, BF16), jnp.asarray(ones, BF16)]
    if sink is not None:
        s2 = sink.astype(F32) * LOG2E
        hi = s2.astype(BF16)
        lo = (s2 - hi.astype(F32)).astype(BF16)
        per_pair = jnp.stack([hi[0::2], lo[0::2], hi[1::2], lo[1::2]], axis=1)
        per_pair = jnp.pad(per_pair, ((0, 0), (0, LANES - 4)))
        k_aux = np.zeros((2, 2 * BAND, LANES), np.float32)
        k_aux[0, 0, 0:2] = 1.0
        k_aux[1, 0, 2:4] = 1.0
        consts += [jnp.repeat(per_pair, BAND, axis=0), jnp.asarray(k_aux, BF16)]
    return consts


def _band_kernel(*refs, n_kvh, has_sink, tile_axis):
    if has_sink:
        (mask_ref, sel_ref, ones_ref, qaux_ref, kaux_ref, q_ref, kc_ref, kp_ref, vc_ref, vp_ref, z_ref,
         y_ref, kbuf, vbuf) = refs
    else:
        mask_ref, sel_ref, ones_ref, q_ref, kc_ref, kp_ref, vc_ref, vp_ref, o_ref, lse_ref, kbuf, vbuf = refs
    n_pair = N_HEADS // n_kvh // 2
    tq = q_ref.shape[0]
    i_tile = pl.program_id(tile_axis)
    kbuf[0:BAND, :] = kp_ref[...]
    kbuf[BAND:BAND + tq, :] = kc_ref[...]
    vbuf[0:BAND, :] = vp_ref[...]
    vbuf[BAND:BAND + tq, :] = vc_ref[...]
    lane = lax.broadcasted_iota(jnp.int32, (BAND, LANES), 1)
    lo = lane < HEAD_DIM

    def block(bb, carry):
        r0 = pl.multiple_of(bb * BAND, BAND)
        mask = mask_ref[jnp.where((i_tile == 0) & (bb == 0), 0, 1)]
        lse_acc = jnp.zeros((BAND, LANES), F32)
        for j in range(n_kvh):
            kcat = kbuf[pl.ds(r0, 2 * BAND), j * LANES:(j + 1) * LANES]
            vcat = vbuf[pl.ds(r0, 2 * BAND), j * LANES:(j + 1) * LANES]
            k_e, k_o = kcat * sel_ref[0], kcat * sel_ref[1]
            v_e = jnp.concatenate([vcat * sel_ref[0], ones_ref[0]], axis=1)
            v_o = jnp.concatenate([vcat * sel_ref[1], ones_ref[1]], axis=1)
            qp = jnp.concatenate(
                [q_ref[pl.ds(r0, BAND), (j * n_pair + pi) * LANES:(j * n_pair + pi + 1) * LANES]
                 for pi in range(n_pair)], axis=0)
            if has_sink:
                qp = jnp.concatenate([qp, qaux_ref[j * n_pair * BAND:(j + 1) * n_pair * BAND, :]], axis=1)
                k_e = jnp.concatenate([k_e, kaux_ref[0]], axis=1)
                k_o = jnp.concatenate([k_o, kaux_ref[1]], axis=1)
            kbd = jnp.concatenate([k_e, k_o], axis=0)
            vbd = jnp.concatenate([v_e, v_o], axis=0)
            s = lax.dot_general(qp, kbd, NT_DIMS, preferred_element_type=F32)
            ps, ms = [], []
            for pi in range(n_pair):
                sp = s[pi * BAND:(pi + 1) * BAND] + mask
                m_e = jnp.max(sp[:, 0:2 * BAND], axis=-1, keepdims=True)
                m_o = jnp.max(sp[:, 2 * BAND:], axis=-1, keepdims=True)
                m_full = jnp.concatenate([jnp.broadcast_to(m_e, (BAND, 2 * BAND)),
                                          jnp.broadcast_to(m_o, (BAND, 2 * BAND))], axis=1)
                ms.append((m_e, m_o))
                ps.append(jnp.exp2(sp - m_full).astype(BF16))
            p = jnp.concatenate(ps, axis=0)
            oext = jnp.dot(p, vbd, preferred_element_type=F32)
            for pi in range(n_pair):
                oe = oext[pi * BAND:(pi + 1) * BAND]
                o_slab = oe[:, 0:LANES] / oe[:, LANES:]
                pair = j * n_pair + pi
                col = pair * LANES
                if has_sink:
                    zf = z_ref[pl.ds(r0, BAND), col:col + LANES].astype(F32)
                    y_ref[pl.ds(r0, BAND), col:col + LANES] = (o_slab * _silu(zf)).astype(BF16)
                else:
                    o_ref[pl.ds(r0, BAND), col:col + LANES] = o_slab.astype(BF16)
                    lse_slab = jnp.where(lo, ms[pi][0], ms[pi][1]) + jnp.log2(oe[:, LANES:])
                    a, b = pair * LSE_LANES, HEAD_DIM + pair * LSE_LANES
                    mine = ((lane >= a) & (lane < a + LSE_LANES)) | ((lane >= b) & (lane < b + LSE_LANES))
                    lse_acc = jnp.where(mine, lse_slab, lse_acc)
        if not has_sink:
            lse_ref[pl.ds(r0, BAND), :] = lse_acc
        return carry

    lax.fori_loop(0, tq // BAND, block, 0, unroll=True)


def _swa_attention(q, kx, vx, z, sink, *, max_dist):
    n_b, seq, _ = q.shape
    kw = kx.shape[2]
    n_kvh = kw // LANES
    tq = 1024
    bpt = tq // BAND
    cur = lambda b, i: (b, i, 0)
    prev = lambda b, i: (b, jnp.maximum(i * bpt - 1, 0), 0)
    consts = _band_consts(max_dist, sink)
    const_specs = [pl.BlockSpec(c.shape, lambda b, i, n=c.ndim: (0,) * n) for c in consts]
    vmem = (2 * tq * D_MODEL * 2 * 3 + 6 * (tq + BAND) * kw * 2 * 2
            + 2 * sum(int(np.prod(c.shape)) * c.dtype.itemsize for c in consts) + 16 * 4 * BAND * 4 * BAND * 4)
    return pl.pallas_call(
        functools.partial(_band_kernel, n_kvh=n_kvh, has_sink=True, tile_axis=1),
        grid=(n_b, seq // tq),
        in_specs=const_specs + [
            pl.BlockSpec((None, tq, D_MODEL), cur),
            pl.BlockSpec((None, tq, kw), cur), pl.BlockSpec((None, BAND, kw), prev),
            pl.BlockSpec((None, tq, kw), cur), pl.BlockSpec((None, BAND, kw), prev),
            pl.BlockSpec((None, tq, D_MODEL), cur)],
        out_specs=pl.BlockSpec((None, tq, D_MODEL), cur),
        out_shape=jax.ShapeDtypeStruct((n_b, seq, D_MODEL), BF16),
        scratch_shapes=[pltpu.VMEM((tq + BAND, kw), BF16), pltpu.VMEM((tq + BAND, kw), BF16)],
        compiler_params=_params(vmem, 2),
        name="swa_attention",
    )(*consts, q, kx, kx, vx, vx, z)


def _dilated_group(q, kx, vx, gi, *, max_dist):
    n_b, dil, fold, _ = q.shape
    kw = kx.shape[3]
    n_kvh = kw // LANES
    tq = min(1024, fold)
    bpt = tq // BAND
    cur = lambda b, r, i: (b, r, i, 0)
    prev = lambda b, r, i: (b, r, jnp.maximum(i * bpt - 1, 0), 0)
    consts = _band_consts(max_dist, None)
    const_specs = [pl.BlockSpec(c.shape, lambda b, r, i, n=c.ndim: (0,) * n) for c in consts]
    vmem = (2 * tq * D_MODEL * 2 * 2 + 2 * tq * LANES * 4 + 6 * (tq + BAND) * kw * 2 * 2
            + 2 * sum(int(np.prod(c.shape)) * c.dtype.itemsize for c in consts) + 16 * 2 * BAND * 4 * BAND * 4)
    return pl.pallas_call(
        functools.partial(_band_kernel, n_kvh=n_kvh, has_sink=False, tile_axis=2),
        grid=(n_b, dil, fold // tq),
        in_specs=const_specs + [
            pl.BlockSpec((None, None, tq, D_MODEL), cur),
            pl.BlockSpec((None, None, tq, kw), cur), pl.BlockSpec((None, None, BAND, kw), prev),
            pl.BlockSpec((None, None, tq, kw), cur), pl.BlockSpec((None, None, BAND, kw), prev)],
        out_specs=[pl.BlockSpec((None, None, tq, D_MODEL), cur), pl.BlockSpec((None, None, tq, LANES), cur)],
        out_shape=[jax.ShapeDtypeStruct((n_b, dil, fold, D_MODEL), BF16),
                   jax.ShapeDtypeStruct((n_b, dil, fold, LANES), F32)],
        scratch_shapes=[pltpu.VMEM((tq + BAND, kw), BF16), pltpu.VMEM((tq + BAND, kw), BF16)],
        compiler_params=_params(vmem, 3),
        name="dilated_group_%d" % gi,
    )(*consts, q, kx, kx, vx, vx)


def _dilated_merge_kernel(*refs, dils):
    n_grp = len(dils)
    it = iter(refs)
    o_refs = [next(it) for _ in range(n_grp)]
    l_refs = [next(it) for _ in range(n_grp)]
    p_refs = [next(it) if d > 1 else None for d in dils]
    e_ref, z_ref, y_ref = next(it), next(it), next(it)
    tc = MERGE_CHUNK

    def split3(a):
        hi = a.astype(BF16)
        r1 = a - hi.astype(F32)
        mid = r1.astype(BF16)
        return hi, mid, (r1 - mid.astype(F32)).astype(BF16)

    for c in range(z_ref.shape[0] // tc):
        outs, lses = [], []
        for d, o_ref, l_ref, p_ref in zip(dils, o_refs, l_refs, p_refs):
            fold = slice(c * tc // d, (c + 1) * tc // d)
            o = o_ref[:, fold, :].reshape(tc, D_MODEL)
            lse = l_ref[:, fold, :].reshape(tc, LANES)
            if d > 1:
                pt = p_ref[...]
                o = jnp.dot(pt, o, preferred_element_type=F32)
                parts = jnp.concatenate(split3(lse), axis=1)
                lp = jnp.dot(pt, parts, preferred_element_type=F32)
                lse = lp[:, 0:LANES] + lp[:, LANES:2 * LANES] + lp[:, 2 * LANES:]
            else:
                o = o.astype(F32)
            outs.append(o)
            lses.append(lse)
        m = functools.reduce(jnp.maximum, lses)
        es = [jnp.exp2(l - m) for l in lses]
        den = functools.reduce(lambda a, b: a + b, es)
        acc = jnp.zeros((tc, D_MODEL), F32)
        for e, o in zip(es, outs):
            hi, mid, _ = split3(e / den)
            w = jnp.dot(jnp.concatenate([hi, mid], axis=1), e_ref[...], preferred_element_type=F32)
            acc = acc + w * o
        rows = slice(c * tc, (c + 1) * tc)
        y_ref[rows, :] = (acc * _silu(z_ref[rows, :].astype(F32))).astype(BF16)


def _dilated_merge(outs, lses, z2, *, n_b, seq, dils):
    tm = 2 * MERGE_CHUNK
    seq_tiles = seq // tm
    expand = np.zeros((LANES, D_MODEL), np.float32)
    for h in range(N_HEADS):
        expand[(h % 2) * HEAD_DIM + (h // 2) * LSE_LANES, h * HEAD_DIM:(h + 1) * HEAD_DIM] = 1.0
    expand = jnp.asarray(np.concatenate([expand, expand], axis=0), BF16)
    omap = lambda i: (i // seq_tiles, 0, i % seq_tiles, 0)
    in_specs = [pl.BlockSpec((None, d, tm // d, D_MODEL), omap) for d in dils]
    in_specs += [pl.BlockSpec((None, d, tm // d, LANES), omap) for d in dils]
    perms = [jnp.asarray(_fold_perm(MERGE_CHUNK, d).T, BF16) for d in dils if d > 1]
    in_specs += [pl.BlockSpec(p.shape, lambda i: (0, 0)) for p in perms]
    in_specs += [pl.BlockSpec(expand.shape, lambda i: (0, 0)), pl.BlockSpec((tm, D_MODEL), lambda i: (i, 0))]
    vmem = (2 * tm * D_MODEL * (3 * 2 + 2 + 2) + 2 * 3 * tm * LANES * 4 + 4 * tm * tm * 2
            + 2 * 2 * LANES * D_MODEL * 2 + 10 * tm * D_MODEL * 4)
    return pl.pallas_call(
        functools.partial(_dilated_merge_kernel, dils=tuple(dils)),
        grid=(n_b * seq // tm,),
        in_specs=in_specs,
        out_specs=pl.BlockSpec((tm, D_MODEL), lambda i: (i, 0)),
        out_shape=jax.ShapeDtypeStruct((n_b * seq, D_MODEL), BF16),
        compiler_params=_params(vmem, 1),
        name="dilated_merge",
    )(*outs, *lses, *perms, expand, z2)


def _moba_kernel(tril_ref, q_ref, kx_ref, vx_ref, z_ref, y_ref,
                 km_scr, qext_scr, sa_scr, sb_scr, m_scr, acc_scr, *, n_blk, n_h):
    t = pl.program_id(2)
    g = 4
    tq = MOBA_BLOCK
    rows_n = g * tq
    lane = lax.broadcasted_iota(jnp.int32, (tq, LANES), 1)
    lo = lane < HEAD_DIM
    heads = range(n_h)

    def kslab(ref, hh, blk_idx):
        r0 = pl.multiple_of(blk_idx * MOBA_BLOCK, MOBA_BLOCK)
        return ref[pl.ds(r0, MOBA_BLOCK), hh * LANES:(hh + 1) * LANES]

    @pl.when(t == 0)
    def _block_means():
        for hh in heads:
            means = []
            for n in range(n_blk):
                kb = kx_ref[n * MOBA_BLOCK:(n + 1) * MOBA_BLOCK, hh * LANES:(hh + 1) * LANES].astype(F32)
                means.append(jnp.sum(kb, axis=0, keepdims=True) * (1.0 / MOBA_BLOCK))
            km = jnp.concatenate(means, axis=0)
            km = jnp.where(lax.broadcasted_iota(jnp.int32, km.shape, 1) < HEAD_DIM, km, 0.0)
            km_scr[hh] = jnp.concatenate(
                [jnp.zeros((HEAD_DIM, LANES), F32), km,
                 jnp.zeros((LANES - HEAD_DIM - n_blk, LANES), F32)], axis=0).astype(BF16)

    tril = tril_ref[...]
    for hh in heads:
        rows = []
        for gi in range(g):
            slab = q_ref[:, (2 * hh + gi // 2) * LANES:(2 * hh + gi // 2 + 1) * LANES].astype(F32)
            if gi % 2:
                slab = pltpu.roll(slab, HEAD_DIM, 1)
            rows.append(jnp.where(lo, slab, 0.0))
        qrows = jnp.concatenate(rows, axis=0)
        qb = qrows.astype(BF16)

        s0 = lax.dot_general(qb, kslab(kx_ref, hh, t), NT_DIMS, preferred_element_type=F32)
        sa_scr[hh] = jnp.concatenate([s0[gi * tq:(gi + 1) * tq] + tril for gi in range(g)], axis=0)
        m_scr[hh] = jnp.full((rows_n, LANES), NEG_BIG, F32)
        acc_scr[hh] = jnp.zeros((rows_n, LANES), F32)

        gate_t = lax.dot_general(km_scr[hh], qb, NT_DIMS, preferred_element_type=F32)
        gate = gate_t[HEAD_DIM:HEAD_DIM + n_blk]
        blk = lax.broadcasted_iota(jnp.int32, gate.shape, 0).astype(F32)
        tf = t.astype(F32)
        gate = jnp.where(blk < tf, gate, -jnp.inf)
        sel = jnp.zeros(gate.shape, F32)
        for _ in range(MOBA_TOPK):
            mx = jnp.max(gate, axis=0, keepdims=True)
            is_max = (gate == mx) & (mx > -jnp.inf)
            first = jnp.min(jnp.where(is_max, blk, float(n_blk)), axis=0, keepdims=True)
            pick = blk == first
            sel = jnp.where(pick, 1.0, sel)
            gate = jnp.where(pick, -jnp.inf, gate)
        bias_t = jnp.where((sel > 0.0) | (blk == tf), 0.0, NEG_BIG)
        bias_full = jnp.concatenate(
            [jnp.zeros((HEAD_DIM, rows_n), F32), bias_t,
             jnp.zeros((LANES - HEAD_DIM - n_blk, rows_n), F32)], axis=0)
        lane_r = lax.broadcasted_iota(jnp.int32, (rows_n, LANES), 1)
        qext_scr[hh] = jnp.where(lane_r < HEAD_DIM, qrows, bias_full.T).astype(BF16)

    def scores(hh, blk_idx):
        return lax.dot_general(qext_scr[hh], kslab(kx_ref, hh, blk_idx), NT_DIMS,
                               preferred_element_type=F32)

    def consume(hh, sn, blk_idx):
        m_old = m_scr[hh]
        m_new = jnp.maximum(m_old, jnp.max(sn, axis=-1, keepdims=True))
        alpha = jnp.exp2(m_old - m_new)
        pn = jnp.exp2(sn - jnp.concatenate([m_new, m_new], axis=1)).astype(BF16)
        acc_scr[hh] = alpha * acc_scr[hh] + jnp.dot(pn, kslab(vx_ref, hh, blk_idx),
                                                    preferred_element_type=F32)
        m_scr[hh] = m_new

    last_past = jnp.maximum(t - 1, 0)

    def pair(kk, carry):
        k0 = 2 * kk
        for hh in heads:
            sb_scr[hh] = scores(hh, jnp.minimum(k0, last_past))
        for hh in heads:
            consume(hh, sa_scr[hh], jnp.where(k0 == 0, t, k0 - 1))
        for hh in heads:
            sa_scr[hh] = scores(hh, jnp.minimum(k0 + 1, last_past))
        pen = jnp.where(k0 < t, 0.0, NEG_BIG)
        for hh in heads:
            consume(hh, sb_scr[hh] + pen, jnp.minimum(k0, last_past))
        return carry

    lax.fori_loop(0, t // 2 + 1, pair, 0)

    for hh in heads:
        acc = acc_scr[hh]
        for pi in range(g // 2):
            oe = acc[(2 * pi) * tq:(2 * pi + 1) * tq]
            oo = acc[(2 * pi + 1) * tq:(2 * pi + 2) * tq]
            o_slab = jnp.where(lo, oe / pltpu.roll(oe, HEAD_DIM, 1), pltpu.roll(oo, HEAD_DIM, 1) / oo)
            col = (2 * hh + pi) * LANES
            zf = z_ref[:, col:col + LANES].astype(F32)
            y_ref[:, col:col + LANES] = (o_slab * _silu(zf)).astype(BF16)


def _moba_attention(q, kx, vx, z):
    n_b, seq, _ = q.shape
    n_kvh = kx.shape[2] // LANES
    n_blk = seq // MOBA_BLOCK
    g = N_HEADS // n_kvh
    n_h = 1
    tq = MOBA_BLOCK
    qw = n_h * g * HEAD_DIM
    tril = jnp.asarray(np.where(np.tril(np.ones((tq, tq), bool)), 0.0, NEG_BIG), F32)
    tile = lambda b, j, t: (b, t, j)
    whole = lambda b, j, t: (b, 0, j)
    vmem = (2 * 2 * seq * n_h * LANES * 2 + 2 * 3 * tq * qw * 2 + tq * tq * 4 * 2
            + n_h * (g * tq * LANES * (2 + 4 + 4) + 2 * g * tq * tq * 4) + 8 * g * tq * tq * 4)
    return pl.pallas_call(
        functools.partial(_moba_kernel, n_blk=n_blk, n_h=n_h),
        grid=(n_b, n_kvh // n_h, seq // tq),
        in_specs=[pl.BlockSpec((tq, tq), lambda b, j, t: (0, 0)),
                  pl.BlockSpec((None, tq, qw), tile),
                  pl.BlockSpec((None, seq, n_h * LANES), whole), pl.BlockSpec((None, seq, n_h * LANES), whole),
                  pl.BlockSpec((None, tq, qw), tile)],
        out_specs=pl.BlockSpec((None, tq, qw), tile),
        out_shape=jax.ShapeDtypeStruct((n_b, seq, D_MODEL), BF16),
        scratch_shapes=[pltpu.VMEM((n_h, LANES, LANES), BF16), pltpu.VMEM((n_h, g * tq, LANES), BF16),
                        pltpu.VMEM((n_h, g * tq, tq), F32), pltpu.VMEM((n_h, g * tq, tq), F32),
                        pltpu.VMEM((n_h, g * tq, LANES), F32), pltpu.VMEM((n_h, g * tq, LANES), F32)],
        compiler_params=_params(vmem, 3),
        name="moba_attention",
    )(tril, q, kx, vx, z)


def _rope_base(seq):
    half = ROT_DIM // 2
    inv = ROPE_THETA ** (-jnp.arange(0, ROT_DIM, 2, dtype=F32) / ROT_DIM)
    ang = jnp.arange(seq, dtype=F32)[:, None] * inv[None, :]
    cos, sin = jnp.cos(ang), jnp.sin(ang)
    rest = HEAD_DIM - ROT_DIM
    zero_h = jnp.zeros((seq, half), F32)
    c64 = jnp.concatenate([cos, cos, jnp.ones((seq, rest), F32)], axis=1)
    a64 = jnp.concatenate([-sin, zero_h, jnp.zeros((seq, rest), F32)], axis=1)
    b64 = jnp.concatenate([zero_h, sin, jnp.zeros((seq, rest), F32)], axis=1)
    return jnp.stack([jnp.concatenate([t, t], axis=1) for t in (c64, a64, b64)])


def _rope_table(tab, dil):
    if dil == 1:
        return tab
    seq = tab.shape[1]
    return tab.reshape(3, seq // TM, TM // dil, dil, LANES).transpose(0, 1, 3, 2, 4).reshape(3, seq, LANES)


def _prep_w_in(w_in, kind):
    kv = KV_HEADS[kind] * HEAD_DIM
    n_grp = len(DILATED_GROUPS) if kind == 1 else 1
    scale = np.ones((w_in.shape[1],), np.float32)
    for g in range(n_grp):
        off = g * (D_MODEL + 2 * kv)
        scale[off:off + D_MODEL] = Q_SCALE
    return (w_in * jnp.asarray(scale)).astype(BF16)


def kernel(x, w_in_0, sink_0, w_out_0, ln_g_0, ln_b_0, w_in_1, w_out_1, ln_g_1, ln_b_1, w_in_2, w_out_2, ln_g_2, ln_b_2, w_in_3, sink_3, w_out_3, ln_g_3, ln_b_3):
    n_b, seq, d_model = x.shape
    assert d_model == D_MODEL and seq % max(dl * BAND for _, dl in DILATED_GROUPS) == 0 and seq % TM == 0
    assert (seq // MOBA_BLOCK) % 8 == 0 and seq // MOBA_BLOCK <= LANES - HEAD_DIM
    layers = [(w_in_0, sink_0, w_out_0, ln_g_0, ln_b_0),
              (w_in_1, None, w_out_1, ln_g_1, ln_b_1),
              (w_in_2, None, w_out_2, ln_g_2, ln_b_2),
              (w_in_3, sink_3, w_out_3, ln_g_3, ln_b_3)]
    dil_b = [dl for _, dl in DILATED_GROUPS]
    rope_nat = _rope_base(seq)
    x2 = x.reshape(n_b * seq, D_MODEL)
    for i, (w_in, sink, w_out, ln_g, ln_b) in enumerate(layers):
        kind = i % N_MIXERS
        w = _prep_w_in(w_in, kind)
        nat = lambda a: a.reshape(n_b, seq, a.shape[-1])
        if kind == 1:
            tabs = [_rope_table(rope_nat, dl) for dl in dil_b]
            grp, z = _inproj(x2, tabs, w, n_b=n_b, seq=seq, dils=dil_b, n_kvh=KV_HEADS[kind], moba=False)
            outs, lses = [], []
            for gi, ((window, dl), (q, kx, vx)) in enumerate(zip(DILATED_GROUPS, grp)):
                o, lse = _dilated_group(q, kx, vx, gi, max_dist=window // dl)
                outs.append(o)
                lses.append(lse)
            y = _dilated_merge(outs, lses, z, n_b=n_b, seq=seq, dils=dil_b)
        else:
            grp, z = _inproj(x2, [rope_nat], w, n_b=n_b, seq=seq, dils=[1], n_kvh=KV_HEADS[kind],
                             moba=(kind == 2))
            q, kx, vx = (nat(a) for a in grp[0])
            if kind == 0:
                y = _swa_attention(q, kx, vx, nat(z), sink, max_dist=WINDOW_A - 1)
            else:
                y = _moba_attention(q, kx, vx, nat(z))
        x2 = _outproj(y.reshape(n_b * seq, D_MODEL), x2, w_out.astype(BF16),
                      ln_g.reshape(1, D_MODEL).astype(F32), ln_b.reshape(1, D_MODEL).astype(F32))
    return x2.reshape(n_b, seq, D_MODEL)
```

```python
import functools
import math

import numpy as np
import jax
import jax.numpy as jnp
from jax import lax
from jax.experimental import pallas as pl
from jax.experimental.pallas import tpu as pltpu

D_MODEL = 1024
HEAD_DIM = 64
N_HEADS = D_MODEL // HEAD_DIM
ROT_DIM = HEAD_DIM // 4
ROPE_THETA = 500000.0
DEPTH = 4
N_MIXERS = 3
KV_HEADS = (2, 4, 4)
WINDOW_A = 128
DILATED_GROUPS = ((128, 1), (512, 4), (2048, 16))
BAND = 128
MOBA_BLOCK = 256
MOBA_TOPK = 3
DEEPNORM_ALPHA = (2 * DEPTH) ** 0.25
LN_EPS = 1e-5
LOG2E = math.log2(math.e)
Q_SCALE = HEAD_DIM ** -0.5 * LOG2E

LANES = 128
LSE_LANES = LANES // N_HEADS
NEG_BIG = -1e30
VMEM_CAP = 60000 * 1024
TM = 512
MERGE_CHUNK = 256
BAND_TILE = 1024
MOBA_UNROLL = 4

F32 = jnp.float32
BF16 = jnp.bfloat16
NT_DIMS = (((1,), (1,)), ((), ()))


def _params(vmem_bytes, n_grid):
    limit = int(min(max(2 * vmem_bytes, 32 * 1024 * 1024), VMEM_CAP))
    return pltpu.CompilerParams(dimension_semantics=("arbitrary",) * n_grid,
                                vmem_limit_bytes=limit)


def _silu(z):
    return z / (1.0 + jnp.exp(-z))


def _fold_perm(n, dil):
    p = np.zeros((n, n), np.float32)
    idx = np.arange(n)
    p[(idx % dil) * (n // dil) + idx // dil, idx] = 1.0
    return p


def _inproj_kernel(*refs, dils, n_kvh, moba, seq_tiles):
    n_grp = len(dils)
    kv = n_kvh * HEAD_DIM
    it = iter(refs)
    x_ref = next(it)
    groups = [(d, next(it) if d > 1 else None, next(it)) for d in dils]
    w_ref = next(it)
    outs = [(next(it), next(it), next(it)) for _ in range(n_grp)]
    z_ref = next(it)

    def proj(lhs, col, width):
        return jnp.dot(lhs, w_ref[:, col:col + width], preferred_element_type=F32)

    tm = x_ref.shape[0]
    xb = x_ref[...].astype(BF16)
    lane = lax.broadcasted_iota(jnp.int32, (tm, LANES), 1)
    lo = lane < HEAD_DIM
    if moba:
        row = lax.broadcasted_iota(jnp.int32, (tm, LANES), 0)
        pos = (pl.program_id(0) % seq_tiles) * tm + row
        k_aux = jnp.where(lane == HEAD_DIM + lax.shift_right_logical(pos, 8), 1.0, 0.0)

    off = 0
    for (d, perm_ref, rope_ref), (q_ref, kx_ref, vx_ref) in zip(groups, outs):
        n = tm // d
        xg = xb if d == 1 else jnp.dot(perm_ref[...], xb, preferred_element_type=F32).astype(BF16)
        cs, sa, sb = rope_ref[0], rope_ref[1], rope_ref[2]

        def rope(t):
            return (t * cs + pltpu.roll(t, LANES - ROT_DIM // 2, 1) * sa
                    + pltpu.roll(t, ROT_DIM // 2, 1) * sb)

        def put(ref, col, val):
            for r in range(d):
                ref[r, :, col:col + val.shape[1]] = val[r * n:(r + 1) * n]

        for c in range(D_MODEL // 512):
            acc = proj(xg, off + c * 512, 512)
            for s in range(4):
                put(q_ref, c * 512 + s * LANES, rope(acc[:, s * LANES:(s + 1) * LANES]).astype(BF16))
        acc_k = proj(xg, off + D_MODEL, kv)
        acc_v = proj(xg, off + D_MODEL + kv, kv)
        off += D_MODEL + 2 * kv
        for c in range(n_kvh // 2):
            k2 = rope(acc_k[:, c * LANES:(c + 1) * LANES])
            v2 = acc_v[:, c * LANES:(c + 1) * LANES]
            k2r = pltpu.roll(k2, HEAD_DIM, 1)
            v2r = pltpu.roll(v2, HEAD_DIM, 1)
            for e in range(2):
                j = 2 * c + e
                if moba:
                    kk = jnp.where(lo, k2 if e == 0 else k2r, k_aux)
                    vv = jnp.where(lo, v2 if e == 0 else v2r, 1.0)
                else:
                    kk = jnp.where(lo, k2, k2r) if e == 0 else jnp.where(lo, k2r, k2)
                    vv = jnp.where(lo, v2, v2r) if e == 0 else jnp.where(lo, v2r, v2)
                put(kx_ref, j * LANES, kk.astype(BF16))
                put(vx_ref, j * LANES, vv.astype(BF16))

    for c in range(D_MODEL // 512):
        z_ref[:, c * 512:(c + 1) * 512] = proj(xb, off + c * 512, 512).astype(BF16)


def _inproj(x2, rope_tabs, w, *, n_b, seq, dils, n_kvh, moba):
    tm = TM
    seq_tiles = seq // tm
    kw = n_kvh * LANES
    const = lambda a: pl.BlockSpec(a.shape, lambda i: (0, 0), pipeline_mode=pl.Buffered(1))
    in_specs = [pl.BlockSpec((tm, D_MODEL), lambda i: (i, 0))]
    args = [x2]
    for d, tab in zip(dils, rope_tabs):
        if d > 1:
            perm = jnp.asarray(_fold_perm(tm, d), BF16)
            in_specs.append(const(perm))
            args.append(perm)
        in_specs.append(pl.BlockSpec((3, tm, LANES), lambda i: (0, i % seq_tiles, 0)))
        args.append(tab)
    in_specs.append(const(w))
    args.append(w)
    out_specs, out_shape = [], []
    for d in dils:
        omap = lambda i: (i // seq_tiles, 0, i % seq_tiles, 0)
        for width in (D_MODEL, kw, kw):
            out_specs.append(pl.BlockSpec((None, d, tm // d, width), omap))
            out_shape.append(jax.ShapeDtypeStruct((n_b, d, seq // d, width), BF16))
    out_specs.append(pl.BlockSpec((tm, D_MODEL), lambda i: (i, 0)))
    out_shape.append(jax.ShapeDtypeStruct((n_b * seq, D_MODEL), BF16))
    vmem = (2 * tm * D_MODEL * 4 + len(dils) * (2 * 3 * tm * LANES * 4 + tm * tm * 2)
            + 2 * D_MODEL * w.shape[1] + 2 * 2 * tm * (len(dils) * (D_MODEL + 2 * kw) + D_MODEL)
            + 6 * tm * 512 * 4)
    res = pl.pallas_call(
        functools.partial(_inproj_kernel, dils=tuple(dils), n_kvh=n_kvh, moba=moba, seq_tiles=seq_tiles),
        grid=(n_b * seq // tm,),
        in_specs=in_specs, out_specs=out_specs, out_shape=out_shape,
        compiler_params=_params(vmem, 1),
        name="inproj",
    )(*args)
    return [tuple(res[3 * g:3 * g + 3]) for g in range(len(dils))], res[-1]


def _outproj_kernel(y_ref, x_ref, w_ref, g_ref, b_ref, o_ref):
    rows = 256
    for c in range(y_ref.shape[0] // rows):
        sl = slice(c * rows, (c + 1) * rows)
        t = DEEPNORM_ALPHA * x_ref[sl, :] + jnp.dot(y_ref[sl, :], w_ref[...], preferred_element_type=F32)
        mu = jnp.mean(t, axis=-1, keepdims=True)
        d = t - mu
        var = jnp.mean(d * d, axis=-1, keepdims=True)
        o_ref[sl, :] = d * lax.rsqrt(var + LN_EPS) * g_ref[...] + b_ref[...]


def _outproj(y2, x2, w, g, b):
    t_rows = x2.shape[0]
    tm = 2 * TM
    row = lambda i: (i, 0)
    full = lambda i: (0, 0)
    vmem = 2 * tm * D_MODEL * (2 + 4 + 4) + 2 * D_MODEL * D_MODEL * 2 + 3 * 128 * D_MODEL * 4
    return pl.pallas_call(
        _outproj_kernel,
        grid=(t_rows // tm,),
        in_specs=[pl.BlockSpec((tm, D_MODEL), row), pl.BlockSpec((tm, D_MODEL), row),
                  pl.BlockSpec((D_MODEL, D_MODEL), full),
                  pl.BlockSpec((1, D_MODEL), full), pl.BlockSpec((1, D_MODEL), full)],
        out_specs=pl.BlockSpec((tm, D_MODEL), row),
        out_shape=jax.ShapeDtypeStruct((t_rows, D_MODEL), F32),
        compiler_params=_params(vmem, 1),
        name="outproj_ln",
    )(y2, x2, w, g, b)


def _band_consts(max_dist, sink):
    i = np.arange(BAND)[:, None]
    c = np.arange(2 * BAND)[None, :]
    dist = BAND + i - c
    ok = (dist >= 0) & (dist <= max_dist)
    first = ok & (c >= BAND)
    sel = np.zeros((2, 2 * BAND, LANES), np.float32)
    sel[0, :, :HEAD_DIM] = 1.0
    sel[1, :, HEAD_DIM:] = 1.0
    ones = sel.copy()
    if sink is not None:
        assert not ok[:, 0].any()
        ok[:, 0] = True
        first[:, 0] = True
        sel[:, 0, :] = 0.0
    mask = np.where(np.stack([first, ok]), 0.0, NEG_BIG)
    consts = [jnp.asarray(np.concatenate([mask, mask], axis=2), F32),
              jnp.asarray(sel, BF16), jnp.asarray(ones, BF16)]
    if sink is not None:
        s2 = sink.astype(F32) * LOG2E
        hi = s2.astype(BF16)
        lo = (s2 - hi.astype(F32)).astype(BF16)
        per_pair = jnp.stack([hi[0::2], lo[0::2], hi[1::2], lo[1::2]], axis=1)
        per_pair = jnp.pad(per_pair, ((0, 0), (0, LANES - 4)))
        k_aux = np.zeros((2, 2 * BAND, LANES), np.float32)
        k_aux[0, 0, 0:2] = 1.0
        k_aux[1, 0, 2:4] = 1.0
        consts += [jnp.repeat(per_pair, BAND, axis=0), jnp.asarray(k_aux, BF16)]
    return consts


def _band_kernel(*refs, n_kvh, has_sink):
    if has_sink:
        (mask_ref, sel_ref, ones_ref, qaux_ref, kaux_ref, q_ref, kc_ref, kp_ref, vc_ref, vp_ref, z_ref,
         y_ref, kbuf, vbuf) = refs
    else:
        mask_ref, sel_ref, ones_ref, q_ref, kc_ref, kp_ref, vc_ref, vp_ref, o_ref, lse_ref, kbuf, vbuf = refs
    n_pair = N_HEADS // n_kvh // 2
    n_seq, tq = q_ref.shape[0], q_ref.shape[1]
    i_tile = pl.program_id(2)
    kbuf[:, 0:BAND, :] = kp_ref[...]
    kbuf[:, BAND:BAND + tq, :] = kc_ref[...]
    vbuf[:, 0:BAND, :] = vp_ref[...]
    vbuf[:, BAND:BAND + tq, :] = vc_ref[...]
    lane = lax.broadcasted_iota(jnp.int32, (BAND, LANES), 1)
    lo = lane < HEAD_DIM
    mask_first = mask_ref[jnp.where(i_tile == 0, 0, 1)]
    mask_rest = mask_ref[1]

    for rr in range(n_seq):
        for bb in range(tq // BAND):
            r0 = bb * BAND
            mask = mask_first if bb == 0 else mask_rest
            lse_acc = jnp.zeros((BAND, LANES), F32)
            for j in range(n_kvh):
                kcat = kbuf[rr, r0:r0 + 2 * BAND, j * LANES:(j + 1) * LANES]
                vcat = vbuf[rr, r0:r0 + 2 * BAND, j * LANES:(j + 1) * LANES]
                k_e, k_o = kcat * sel_ref[0], kcat * sel_ref[1]
                v_e = jnp.concatenate([vcat * sel_ref[0], ones_ref[0]], axis=1)
                v_o = jnp.concatenate([vcat * sel_ref[1], ones_ref[1]], axis=1)
                qp = jnp.concatenate(
                    [q_ref[rr, r0:r0 + BAND, (j * n_pair + pi) * LANES:(j * n_pair + pi + 1) * LANES]
                     for pi in range(n_pair)], axis=0)
                if has_sink:
                    qp = jnp.concatenate([qp, qaux_ref[j * n_pair * BAND:(j + 1) * n_pair * BAND, :]], axis=1)
                    k_e = jnp.concatenate([k_e, kaux_ref[0]], axis=1)
                    k_o = jnp.concatenate([k_o, kaux_ref[1]], axis=1)
                kbd = jnp.concatenate([k_e, k_o], axis=0)
                vbd = jnp.concatenate([v_e, v_o], axis=0)
                s = lax.dot_general(qp, kbd, NT_DIMS, preferred_element_type=F32)
                ps, ms = [], []
                for pi in range(n_pair):
                    sp = s[pi * BAND:(pi + 1) * BAND] + mask
                    m_e = jnp.max(sp[:, 0:2 * BAND], axis=-1, keepdims=True)
                    m_o = jnp.max(sp[:, 2 * BAND:], axis=-1, keepdims=True)
                    m_full = jnp.concatenate([jnp.broadcast_to(m_e, (BAND, 2 * BAND)),
                                              jnp.broadcast_to(m_o, (BAND, 2 * BAND))], axis=1)
                    ms.append((m_e, m_o))
                    ps.append(jnp.exp2(sp - m_full).astype(BF16))
                p = jnp.concatenate(ps, axis=0)
                oext = jnp.dot(p, vbd, preferred_element_type=F32)
                for pi in range(n_pair):
                    oe = oext[pi * BAND:(pi + 1) * BAND]
                    o_slab = oe[:, 0:LANES] / oe[:, LANES:]
                    pair = j * n_pair + pi
                    col = pair * LANES
                    if has_sink:
                        zf = z_ref[rr, r0:r0 + BAND, col:col + LANES].astype(F32)
                        y_ref[rr, r0:r0 + BAND, col:col + LANES] = (o_slab * _silu(zf)).astype(BF16)
                    else:
                        o_ref[rr, r0:r0 + BAND, col:col + LANES] = o_slab.astype(BF16)
                        lse_slab = jnp.where(lo, ms[pi][0], ms[pi][1]) + jnp.log2(oe[:, LANES:])
                        a, b = pair * LSE_LANES, HEAD_DIM + pair * LSE_LANES
                        mine = ((lane >= a) & (lane < a + LSE_LANES)) | ((lane >= b) & (lane < b + LSE_LANES))
                        lse_acc = jnp.where(mine, lse_slab, lse_acc)
            if not has_sink:
                lse_ref[rr, r0:r0 + BAND, :] = lse_acc


def _band_attention(q, kx, vx, *, max_dist, sink=None, z=None, name):
    n_b, n_seq, length, _ = q.shape
    kw = kx.shape[3]
    n_kvh = kw // LANES
    tq = min(BAND_TILE, length)
    per_step = min(n_seq, BAND_TILE // tq)
    bpt = tq // BAND
    cur = lambda b, r, i: (b, r, i, 0)
    prev = lambda b, r, i: (b, r, jnp.maximum(i * bpt - 1, 0), 0)
    consts = _band_consts(max_dist, sink)
    const_specs = [pl.BlockSpec(c.shape, lambda b, r, i, n=c.ndim: (0,) * n) for c in consts]
    wide = pl.BlockSpec((None, per_step, tq, D_MODEL), cur)
    in_specs = const_specs + [
        wide,
        pl.BlockSpec((None, per_step, tq, kw), cur), pl.BlockSpec((None, per_step, BAND, kw), prev),
        pl.BlockSpec((None, per_step, tq, kw), cur), pl.BlockSpec((None, per_step, BAND, kw), prev)]
    args = [*consts, q, kx, kx, vx, vx]
    shape = (n_b, n_seq, length, D_MODEL)
    if sink is not None:
        in_specs.append(wide)
        args.append(z)
        out_specs, out_shape = wide, jax.ShapeDtypeStruct(shape, BF16)
    else:
        out_specs = [wide, pl.BlockSpec((None, per_step, tq, LANES), cur)]
        out_shape = [jax.ShapeDtypeStruct(shape, BF16),
                     jax.ShapeDtypeStruct((n_b, n_seq, length, LANES), F32)]
    rows = per_step * tq
    vmem = (2 * rows * D_MODEL * 2 * 3 + 2 * rows * LANES * 4 + 6 * per_step * (tq + BAND) * kw * 2 * 2
            + 2 * sum(int(np.prod(c.shape)) * c.dtype.itemsize for c in consts) + 16 * 4 * BAND * 4 * BAND * 4)
    return pl.pallas_call(
        functools.partial(_band_kernel, n_kvh=n_kvh, has_sink=sink is not None),
        grid=(n_b, n_seq // per_step, length // tq),
        in_specs=in_specs, out_specs=out_specs, out_shape=out_shape,
        scratch_shapes=[pltpu.VMEM((per_step, tq + BAND, kw), BF16), pltpu.VMEM((per_step, tq + BAND, kw), BF16)],
        compiler_params=_params(vmem, 3),
        name=name,
    )(*args)


def _dilated_merge_kernel(*refs, dils):
    n_grp = len(dils)
    it = iter(refs)
    o_refs = [next(it) for _ in range(n_grp)]
    l_refs = [next(it) for _ in range(n_grp)]
    p_refs = [next(it) if d > 1 else None for d in dils]
    e_ref, z_ref, y_ref = next(it), next(it), next(it)
    tc = MERGE_CHUNK

    def split3(a):
        hi = a.astype(BF16)
        r1 = a - hi.astype(F32)
        mid = r1.astype(BF16)
        return hi, mid, (r1 - mid.astype(F32)).astype(BF16)

    for c in range(z_ref.shape[0] // tc):
        outs, lses = [], []
        for d, o_ref, l_ref, p_ref in zip(dils, o_refs, l_refs, p_refs):
            fold = slice(c * tc // d, (c + 1) * tc // d)
            o = o_ref[:, fold, :].reshape(tc, D_MODEL)
            lse = l_ref[:, fold, :].reshape(tc, LANES)
            if d > 1:
                pt = p_ref[...]
                o = jnp.dot(pt, o, preferred_element_type=F32)
                parts = jnp.concatenate(split3(lse), axis=1)
                lp = jnp.dot(pt, parts, preferred_element_type=F32)
                lse = lp[:, 0:LANES] + lp[:, LANES:2 * LANES] + lp[:, 2 * LANES:]
            else:
                o = o.astype(F32)
            outs.append(o)
            lses.append(lse)
        m = functools.reduce(jnp.maximum, lses)
        es = [jnp.exp2(l - m) for l in lses]
        den = functools.reduce(lambda a, b: a + b, es)
        acc = jnp.zeros((tc, D_MODEL), F32)
        for e, o in zip(es, outs):
            hi, mid, _ = split3(e / den)
            w = jnp.dot(jnp.concatenate([hi, mid], axis=1), e_ref[...], preferred_element_type=F32)
            acc = acc + w * o
        rows = slice(c * tc, (c + 1) * tc)
        y_ref[rows, :] = (acc * _silu(z_ref[rows, :].astype(F32))).astype(BF16)


def _dilated_merge(outs, lses, z2, *, n_b, seq, dils):
    tm = 2 * MERGE_CHUNK
    seq_tiles = seq // tm
    expand = np.zeros((LANES, D_MODEL), np.float32)
    for h in range(N_HEADS):
        expand[(h % 2) * HEAD_DIM + (h // 2) * LSE_LANES, h * HEAD_DIM:(h + 1) * HEAD_DIM] = 1.0
    expand = jnp.asarray(np.concatenate([expand, expand], axis=0), BF16)
    omap = lambda i: (i // seq_tiles, 0, i % seq_tiles, 0)
    in_specs = [pl.BlockSpec((None, d, tm // d, D_MODEL), omap) for d in dils]
    in_specs += [pl.BlockSpec((None, d, tm // d, LANES), omap) for d in dils]
    perms = [jnp.asarray(_fold_perm(MERGE_CHUNK, d).T, BF16) for d in dils if d > 1]
    in_specs += [pl.BlockSpec(p.shape, lambda i: (0, 0)) for p in perms]
    in_specs += [pl.BlockSpec(expand.shape, lambda i: (0, 0)), pl.BlockSpec((tm, D_MODEL), lambda i: (i, 0))]
    vmem = (2 * tm * D_MODEL * (3 * 2 + 2 + 2) + 2 * 3 * tm * LANES * 4 + 4 * tm * tm * 2
            + 2 * 2 * LANES * D_MODEL * 2 + 10 * tm * D_MODEL * 4)
    return pl.pallas_call(
        functools.partial(_dilated_merge_kernel, dils=tuple(dils)),
        grid=(n_b * seq // tm,),
        in_specs=in_specs,
        out_specs=pl.BlockSpec((tm, D_MODEL), lambda i: (i, 0)),
        out_shape=jax.ShapeDtypeStruct((n_b * seq, D_MODEL), BF16),
        compiler_params=_params(vmem, 1),
        name="dilated_merge",
    )(*outs, *lses, *perms, expand, z2)


def _moba_kernel(tril_ref, q_ref, kx_ref, vx_ref, z_ref, y_ref,
                 km_scr, qext_scr, sa_scr, sb_scr, m_scr, acc_scr, *, n_blk, n_h):
    t = pl.program_id(2)
    g = 4
    tq = MOBA_BLOCK
    rows_n = g * tq
    lane = lax.broadcasted_iota(jnp.int32, (tq, LANES), 1)
    lo = lane < HEAD_DIM
    heads = range(n_h)

    def kslab(ref, hh, blk_idx):
        r0 = pl.multiple_of(blk_idx * MOBA_BLOCK, MOBA_BLOCK)
        return ref[pl.ds(r0, MOBA_BLOCK), hh * LANES:(hh + 1) * LANES]

    @pl.when(t == 0)
    def _block_means():
        for hh in heads:
            means = []
            for n in range(n_blk):
                kb = kx_ref[n * MOBA_BLOCK:(n + 1) * MOBA_BLOCK, hh * LANES:(hh + 1) * LANES].astype(F32)
                means.append(jnp.sum(kb, axis=0, keepdims=True) * (1.0 / MOBA_BLOCK))
            km = jnp.concatenate(means, axis=0)
            km = jnp.where(lax.broadcasted_iota(jnp.int32, km.shape, 1) < HEAD_DIM, km, 0.0)
            km_scr[hh] = jnp.concatenate(
                [jnp.zeros((HEAD_DIM, LANES), F32), km,
                 jnp.zeros((LANES - HEAD_DIM - n_blk, LANES), F32)], axis=0).astype(BF16)

    tril = tril_ref[...]
    for hh in heads:
        rows = []
        for gi in range(g):
            slab = q_ref[:, (2 * hh + gi // 2) * LANES:(2 * hh + gi // 2 + 1) * LANES].astype(F32)
            if gi % 2:
                slab = pltpu.roll(slab, HEAD_DIM, 1)
            rows.append(jnp.where(lo, slab, 0.0))
        qrows = jnp.concatenate(rows, axis=0)
        qb = qrows.astype(BF16)

        s0 = lax.dot_general(qb, kslab(kx_ref, hh, t), NT_DIMS, preferred_element_type=F32)
        sa_scr[hh] = jnp.concatenate([s0[gi * tq:(gi + 1) * tq] + tril for gi in range(g)], axis=0)
        m_scr[hh] = jnp.full((rows_n, LANES), NEG_BIG, F32)
        acc_scr[hh] = jnp.zeros((rows_n, LANES), F32)

        gate_t = lax.dot_general(km_scr[hh], qb, NT_DIMS, preferred_element_type=F32)
        gate = gate_t[HEAD_DIM:HEAD_DIM + n_blk]
        blk = lax.broadcasted_iota(jnp.int32, gate.shape, 0).astype(F32)
        tf = t.astype(F32)
        gate = jnp.where(blk < tf, gate, -jnp.inf)
        sel = jnp.zeros(gate.shape, F32)
        for _ in range(MOBA_TOPK):
            mx = jnp.max(gate, axis=0, keepdims=True)
            is_max = (gate == mx) & (mx > -jnp.inf)
            first = jnp.min(jnp.where(is_max, blk, float(n_blk)), axis=0, keepdims=True)
            pick = blk == first
            sel = jnp.where(pick, 1.0, sel)
            gate = jnp.where(pick, -jnp.inf, gate)
        bias_t = jnp.where((sel > 0.0) | (blk == tf), 0.0, NEG_BIG)
        bias_full = jnp.concatenate(
            [jnp.zeros((HEAD_DIM, rows_n), F32), bias_t,
             jnp.zeros((LANES - HEAD_DIM - n_blk, rows_n), F32)], axis=0)
        lane_r = lax.broadcasted_iota(jnp.int32, (rows_n, LANES), 1)
        qext_scr[hh] = jnp.where(lane_r < HEAD_DIM, qrows, bias_full.T).astype(BF16)

    def scores(hh, blk_idx):
        return lax.dot_general(qext_scr[hh], kslab(kx_ref, hh, blk_idx), NT_DIMS,
                               preferred_element_type=F32)

    def consume(hh, sn, blk_idx):
        m_old = m_scr[hh]
        m_new = jnp.maximum(m_old, jnp.max(sn, axis=-1, keepdims=True))
        alpha = jnp.exp2(m_old - m_new)
        pn = jnp.exp2(sn - jnp.concatenate([m_new, m_new], axis=1)).astype(BF16)
        acc_scr[hh] = alpha * acc_scr[hh] + jnp.dot(pn, kslab(vx_ref, hh, blk_idx),
                                                    preferred_element_type=F32)
        m_scr[hh] = m_new

    last_past = jnp.maximum(t - 1, 0)

    def stage(src, dst, k):
        if dst is not None:
            for hh in heads:
                dst[hh] = scores(hh, jnp.minimum(k, last_past))
        for hh in heads:
            consume(hh, src[hh], jnp.where(k == 0, t, k - 1))

    def quad(qq, carry):
        k0 = MOBA_UNROLL * qq
        for u in range(MOBA_UNROLL):
            src, dst = (sa_scr, sb_scr) if u % 2 == 0 else (sb_scr, sa_scr)
            stage(src, dst, k0 + u)
        return carry

    n_items = t + 1
    lax.fori_loop(0, n_items // MOBA_UNROLL, quad, 0)
    rem = n_items % MOBA_UNROLL
    base = n_items - rem
    for u in range(MOBA_UNROLL - 1):
        src, dst = (sa_scr, sb_scr) if u % 2 == 0 else (sb_scr, sa_scr)
        last = u == MOBA_UNROLL - 2
        pl.when(rem > u)(functools.partial(stage, src, None if last else dst, base + u))

    for hh in heads:
        acc = acc_scr[hh]
        for pi in range(g // 2):
            oe = acc[(2 * pi) * tq:(2 * pi + 1) * tq]
            oo = acc[(2 * pi + 1) * tq:(2 * pi + 2) * tq]
            o_slab = jnp.where(lo, oe / pltpu.roll(oe, HEAD_DIM, 1), pltpu.roll(oo, HEAD_DIM, 1) / oo)
            col = (2 * hh + pi) * LANES
            zf = z_ref[:, col:col + LANES].astype(F32)
            y_ref[:, col:col + LANES] = (o_slab * _silu(zf)).astype(BF16)


def _moba_attention(q, kx, vx, z):
    n_b, seq, _ = q.shape
    n_kvh = kx.shape[2] // LANES
    n_blk = seq // MOBA_BLOCK
    g = N_HEADS // n_kvh
    n_h = 1
    tq = MOBA_BLOCK
    qw = n_h * g * HEAD_DIM
    tril = jnp.asarray(np.where(np.tril(np.ones((tq, tq), bool)), 0.0, NEG_BIG), F32)
    tile = lambda b, j, t: (b, t, j)
    whole = lambda b, j, t: (b, 0, j)
    vmem = (2 * 2 * seq * n_h * LANES * 2 + 2 * 3 * tq * qw * 2 + tq * tq * 4 * 2
            + n_h * (g * tq * LANES * (2 + 4 + 4) + 2 * g * tq * tq * 4) + 8 * g * tq * tq * 4)
    return pl.pallas_call(
        functools.partial(_moba_kernel, n_blk=n_blk, n_h=n_h),
        grid=(n_b, n_kvh // n_h, seq // tq),
        in_specs=[pl.BlockSpec((tq, tq), lambda b, j, t: (0, 0)),
                  pl.BlockSpec((None, tq, qw), tile),
                  pl.BlockSpec((None, seq, n_h * LANES), whole), pl.BlockSpec((None, seq, n_h * LANES), whole),
                  pl.BlockSpec((None, tq, qw), tile)],
        out_specs=pl.BlockSpec((None, tq, qw), tile),
        out_shape=jax.ShapeDtypeStruct((n_b, seq, D_MODEL), BF16),
        scratch_shapes=[pltpu.VMEM((n_h, LANES, LANES), BF16), pltpu.VMEM((n_h, g * tq, LANES), BF16),
                        pltpu.VMEM((n_h, g * tq, tq), F32), pltpu.VMEM((n_h, g * tq, tq), F32),
                        pltpu.VMEM((n_h, g * tq, LANES), F32), pltpu.VMEM((n_h, g * tq, LANES), F32)],
        compiler_params=_params(vmem, 3),
        name="moba_attention",
    )(tril, q, kx, vx, z)


def _rope_base(seq):
    half = ROT_DIM // 2
    inv = ROPE_THETA ** (-jnp.arange(0, ROT_DIM, 2, dtype=F32) / ROT_DIM)
    c = np.arange(LANES) % HEAD_DIM
    rot = c < ROT_DIM
    freq = jnp.where(rot, inv[c % half], 0.0)
    ang = jnp.arange(seq, dtype=F32)[:, None] * freq[None, :]
    cos, sin = jnp.cos(ang), jnp.sin(ang)
    first = jnp.asarray(c < half, F32)
    second = jnp.asarray(rot & (c >= half), F32)
    return jnp.stack([cos, -sin * first, sin * second])


def _rope_table(tab, dil):
    if dil == 1:
        return tab
    seq = tab.shape[1]
    return tab.reshape(3, seq // TM, TM // dil, dil, LANES).transpose(0, 1, 3, 2, 4).reshape(3, seq, LANES)


def _prep_w_in(w_in, kind):
    kv = KV_HEADS[kind] * HEAD_DIM
    n_grp = len(DILATED_GROUPS) if kind == 1 else 1
    scale = np.ones((w_in.shape[1],), np.float32)
    for g in range(n_grp):
        off = g * (D_MODEL + 2 * kv)
        scale[off:off + D_MODEL] = Q_SCALE
    return (w_in * jnp.asarray(scale)).astype(BF16)


def kernel(x, w_in_0, sink_0, w_out_0, ln_g_0, ln_b_0, w_in_1, w_out_1, ln_g_1, ln_b_1, w_in_2, w_out_2, ln_g_2, ln_b_2, w_in_3, sink_3, w_out_3, ln_g_3, ln_b_3):
    n_b, seq, d_model = x.shape
    assert d_model == D_MODEL and seq % max(dl * BAND for _, dl in DILATED_GROUPS) == 0 and seq % TM == 0
    assert (seq // MOBA_BLOCK) % 8 == 0 and seq // MOBA_BLOCK <= LANES - HEAD_DIM
    layers = [(w_in_0, sink_0, w_out_0, ln_g_0, ln_b_0),
              (w_in_1, None, w_out_1, ln_g_1, ln_b_1),
              (w_in_2, None, w_out_2, ln_g_2, ln_b_2),
              (w_in_3, sink_3, w_out_3, ln_g_3, ln_b_3)]
    dil_b = [dl for _, dl in DILATED_GROUPS]
    rope_nat = _rope_base(seq)
    x2 = x.reshape(n_b * seq, D_MODEL)
    for i, (w_in, sink, w_out, ln_g, ln_b) in enumerate(layers):
        kind = i % N_MIXERS
        w = _prep_w_in(w_in, kind)
        nat = lambda a: a.reshape(n_b, seq, a.shape[-1])
        if kind == 1:
            tabs = [_rope_table(rope_nat, dl) for dl in dil_b]
            grp, z = _inproj(x2, tabs, w, n_b=n_b, seq=seq, dils=dil_b, n_kvh=KV_HEADS[kind], moba=False)
            outs, lses = [], []
            for gi, ((window, dl), (q, kx, vx)) in enumerate(zip(DILATED_GROUPS, grp)):
                o, lse = _band_attention(q, kx, vx, max_dist=window // dl, name="dilated_group_%d" % gi)
                outs.append(o)
                lses.append(lse)
            y = _dilated_merge(outs, lses, z, n_b=n_b, seq=seq, dils=dil_b)
        else:
            grp, z = _inproj(x2, [rope_nat], w, n_b=n_b, seq=seq, dils=[1], n_kvh=KV_HEADS[kind],
                             moba=(kind == 2))
            q, kx, vx = grp[0]
            if kind == 0:
                y = _band_attention(q, kx, vx, max_dist=WINDOW_A - 1, sink=sink,
                                    z=z.reshape(n_b, 1, seq, D_MODEL), name="swa_attention")
            else:
                y = _moba_attention(nat(q), nat(kx), nat(vx), nat(z))
        x2 = _outproj(y.reshape(n_b * seq, D_MODEL), x2, w_out.astype(BF16),
                      ln_g.reshape(1, D_MODEL).astype(F32), ln_b.reshape(1, D_MODEL).astype(F32))
    return x2.reshape(n_b, seq, D_MODEL)
```

```python
import functools
import math

import numpy as np
import jax
import jax.numpy as jnp
from jax import lax
from jax.experimental import pallas as pl
from jax.experimental.pallas import tpu as pltpu

D_MODEL = 1024
HEAD_DIM = 64
N_HEADS = D_MODEL // HEAD_DIM
ROT_DIM = HEAD_DIM // 4
ROPE_THETA = 500000.0
DEPTH = 4
N_MIXERS = 3
KV_HEADS = (2, 4, 4)
WINDOW_A = 128
DILATED_GROUPS = ((128, 1), (512, 4), (2048, 16))
BAND = 128
MOBA_BLOCK = 256
MOBA_TOPK = 3
DEEPNORM_ALPHA = (2 * DEPTH) ** 0.25
LN_EPS = 1e-5
LOG2E = math.log2(math.e)
Q_SCALE = HEAD_DIM ** -0.5 * LOG2E

LANES = 128
LSE_LANES = LANES // N_HEADS
NEG_BIG = -1e30
VMEM_CAP = 60000 * 1024
TM = 512
MERGE_CHUNK = 256
BAND_TILE = 1024
MOBA_UNROLL = 4

F32 = jnp.float32
BF16 = jnp.bfloat16
NT_DIMS = (((1,), (1,)), ((), ()))


def _params(vmem_bytes, n_grid):
    limit = int(min(max(2 * vmem_bytes, 32 * 1024 * 1024), VMEM_CAP))
    return pltpu.CompilerParams(dimension_semantics=("arbitrary",) * n_grid,
                                vmem_limit_bytes=limit)


def _silu(z):
    return z / (1.0 + jnp.exp(-z))


def _fold_perm(n, dil):
    p = np.zeros((n, n), np.float32)
    idx = np.arange(n)
    p[(idx % dil) * (n // dil) + idx // dil, idx] = 1.0
    return p


def _inproj_kernel(*refs, dils, n_kvh, moba, seq_tiles):
    n_grp = len(dils)
    kv = n_kvh * HEAD_DIM
    it = iter(refs)
    x_ref = next(it)
    groups = [(d, next(it) if d > 1 else None, next(it)) for d in dils]
    w_ref = next(it)
    outs = [(next(it), next(it), next(it)) for _ in range(n_grp)]
    z_ref = next(it)

    def proj(lhs, col, width):
        return jnp.dot(lhs, w_ref[:, col:col + width], preferred_element_type=F32)

    tm = x_ref.shape[0]
    xb = x_ref[...].astype(BF16)
    lane = lax.broadcasted_iota(jnp.int32, (tm, LANES), 1)
    lo = lane < HEAD_DIM
    if moba:
        row = lax.broadcasted_iota(jnp.int32, (tm, LANES), 0)
        pos = (pl.program_id(0) % seq_tiles) * tm + row
        k_aux = jnp.where(lane == HEAD_DIM + lax.shift_right_logical(pos, 8), 1.0, 0.0)

    off = 0
    for (d, perm_ref, rope_ref), (q_ref, kx_ref, vx_ref) in zip(groups, outs):
        n = tm // d
        xg = xb if d == 1 else jnp.dot(perm_ref[...], xb, preferred_element_type=F32).astype(BF16)
        cs, sa, sb = rope_ref[0], rope_ref[1], rope_ref[2]

        def rope(t):
            return (t * cs + pltpu.roll(t, LANES - ROT_DIM // 2, 1) * sa
                    + pltpu.roll(t, ROT_DIM // 2, 1) * sb)

        def put(ref, col, val):
            for r in range(d):
                ref[r, :, col:col + val.shape[1]] = val[r * n:(r + 1) * n]

        for c in range(D_MODEL // 512):
            acc = proj(xg, off + c * 512, 512)
            for s in range(4):
                q2 = rope(acc[:, s * LANES:(s + 1) * LANES])
                col = c * 512 + s * LANES
                if moba:
                    put(q_ref, 2 * col, jnp.where(lo, q2, 0.0).astype(BF16))
                    put(q_ref, 2 * col + LANES, jnp.where(lo, pltpu.roll(q2, HEAD_DIM, 1), 0.0).astype(BF16))
                else:
                    put(q_ref, col, q2.astype(BF16))
        acc_k = proj(xg, off + D_MODEL, kv)
        acc_v = proj(xg, off + D_MODEL + kv, kv)
        off += D_MODEL + 2 * kv
        for c in range(n_kvh // 2):
            k2 = rope(acc_k[:, c * LANES:(c + 1) * LANES])
            v2 = acc_v[:, c * LANES:(c + 1) * LANES]
            k2r = pltpu.roll(k2, HEAD_DIM, 1)
            v2r = pltpu.roll(v2, HEAD_DIM, 1)
            for e in range(2):
                j = 2 * c + e
                if moba:
                    kk = jnp.where(lo, k2 if e == 0 else k2r, k_aux)
                    vv = jnp.where(lo, v2 if e == 0 else v2r, 1.0)
                else:
                    kk = jnp.where(lo, k2, k2r) if e == 0 else jnp.where(lo, k2r, k2)
                    vv = jnp.where(lo, v2, v2r) if e == 0 else jnp.where(lo, v2r, v2)
                put(kx_ref, j * LANES, kk.astype(BF16))
                put(vx_ref, j * LANES, vv.astype(BF16))

    for c in range(D_MODEL // 512):
        z_ref[:, c * 512:(c + 1) * 512] = proj(xb, off + c * 512, 512).astype(BF16)


def _inproj(x2, rope_tabs, w, *, n_b, seq, dils, n_kvh, moba):
    tm = TM
    seq_tiles = seq // tm
    kw = n_kvh * LANES
    const = lambda a: pl.BlockSpec(a.shape, lambda i: (0, 0), pipeline_mode=pl.Buffered(1))
    in_specs = [pl.BlockSpec((tm, D_MODEL), lambda i: (i, 0))]
    args = [x2]
    for d, tab in zip(dils, rope_tabs):
        if d > 1:
            perm = jnp.asarray(_fold_perm(tm, d), BF16)
            in_specs.append(const(perm))
            args.append(perm)
        in_specs.append(pl.BlockSpec((3, tm, LANES), lambda i: (0, i % seq_tiles, 0)))
        args.append(tab)
    in_specs.append(const(w))
    args.append(w)
    out_specs, out_shape = [], []
    for d in dils:
        omap = lambda i: (i // seq_tiles, 0, i % seq_tiles, 0)
        for width in (2 * D_MODEL if moba else D_MODEL, kw, kw):
            out_specs.append(pl.BlockSpec((None, d, tm // d, width), omap))
            out_shape.append(jax.ShapeDtypeStruct((n_b, d, seq // d, width), BF16))
    out_specs.append(pl.BlockSpec((tm, D_MODEL), lambda i: (i, 0)))
    out_shape.append(jax.ShapeDtypeStruct((n_b * seq, D_MODEL), BF16))
    vmem = (2 * tm * D_MODEL * 4 + len(dils) * (2 * 3 * tm * LANES * 4 + tm * tm * 2)
            + 2 * D_MODEL * w.shape[1] + 2 * 2 * tm * (len(dils) * (2 * D_MODEL + 2 * kw) + D_MODEL)
            + 6 * tm * 512 * 4)
    res = pl.pallas_call(
        functools.partial(_inproj_kernel, dils=tuple(dils), n_kvh=n_kvh, moba=moba, seq_tiles=seq_tiles),
        grid=(n_b * seq // tm,),
        in_specs=in_specs, out_specs=out_specs, out_shape=out_shape,
        compiler_params=_params(vmem, 1),
        name="inproj",
    )(*args)
    return [tuple(res[3 * g:3 * g + 3]) for g in range(len(dils))], res[-1]


def _outproj_kernel(y_ref, x_ref, w_ref, g_ref, b_ref, o_ref):
    rows = 256
    for c in range(y_ref.shape[0] // rows):
        sl = slice(c * rows, (c + 1) * rows)
        t = DEEPNORM_ALPHA * x_ref[sl, :] + jnp.dot(y_ref[sl, :], w_ref[...], preferred_element_type=F32)
        mu = jnp.mean(t, axis=-1, keepdims=True)
        d = t - mu
        var = jnp.mean(d * d, axis=-1, keepdims=True)
        o_ref[sl, :] = d * lax.rsqrt(var + LN_EPS) * g_ref[...] + b_ref[...]


def _outproj(y2, x2, w, g, b):
    t_rows = x2.shape[0]
    tm = 2 * TM
    row = lambda i: (i, 0)
    full = lambda i: (0, 0)
    vmem = 2 * tm * D_MODEL * (2 + 4 + 4) + 2 * D_MODEL * D_MODEL * 2 + 3 * 128 * D_MODEL * 4
    return pl.pallas_call(
        _outproj_kernel,
        grid=(t_rows // tm,),
        in_specs=[pl.BlockSpec((tm, D_MODEL), row), pl.BlockSpec((tm, D_MODEL), row),
                  pl.BlockSpec((D_MODEL, D_MODEL), full),
                  pl.BlockSpec((1, D_MODEL), full), pl.BlockSpec((1, D_MODEL), full)],
        out_specs=pl.BlockSpec((tm, D_MODEL), row),
        out_shape=jax.ShapeDtypeStruct((t_rows, D_MODEL), F32),
        compiler_params=_params(vmem, 1),
        name="outproj_ln",
    )(y2, x2, w, g, b)


def _band_consts(max_dist, sink):
    i = np.arange(BAND)[:, None]
    c = np.arange(2 * BAND)[None, :]
    dist = BAND + i - c
    ok = (dist >= 0) & (dist <= max_dist)
    first = ok & (c >= BAND)
    sel = np.zeros((2, 2 * BAND, LANES), np.float32)
    sel[0, :, :HEAD_DIM] = 1.0
    sel[1, :, HEAD_DIM:] = 1.0
    ones = sel.copy()
    if sink is not None:
        assert not ok[:, 0].any()
        ok[:, 0] = True
        first[:, 0] = True
        sel[:, 0, :] = 0.0
    mask = np.where(np.stack([first, ok]), 0.0, NEG_BIG)
    consts = [jnp.asarray(np.concatenate([mask, mask], axis=2), F32),
              jnp.asarray(sel, BF16), jnp.asarray(ones, BF16)]
    if sink is not None:
        s2 = sink.astype(F32) * LOG2E
        hi = s2.astype(BF16)
        lo = (s2 - hi.astype(F32)).astype(BF16)
        per_pair = jnp.stack([hi[0::2], lo[0::2], hi[1::2], lo[1::2]], axis=1)
        per_pair = jnp.pad(per_pair, ((0, 0), (0, LANES - 4)))
        k_aux = np.zeros((2, 2 * BAND, LANES), np.float32)
        k_aux[0, 0, 0:2] = 1.0
        k_aux[1, 0, 2:4] = 1.0
        consts += [jnp.repeat(per_pair, BAND, axis=0), jnp.asarray(k_aux, BF16)]
    return consts


def _band_kernel(*refs, n_kvh, has_sink):
    if has_sink:
        (mask_ref, sel_ref, ones_ref, qaux_ref, kaux_ref, q_ref, kc_ref, kp_ref, vc_ref, vp_ref, z_ref,
         y_ref, kbuf, vbuf) = refs
    else:
        mask_ref, sel_ref, ones_ref, q_ref, kc_ref, kp_ref, vc_ref, vp_ref, o_ref, lse_ref, kbuf, vbuf = refs
    n_pair = N_HEADS // n_kvh // 2
    n_seq, tq = q_ref.shape[0], q_ref.shape[1]
    i_tile = pl.program_id(2)
    kbuf[:, 0:BAND, :] = kp_ref[...]
    kbuf[:, BAND:BAND + tq, :] = kc_ref[...]
    vbuf[:, 0:BAND, :] = vp_ref[...]
    vbuf[:, BAND:BAND + tq, :] = vc_ref[...]
    lane = lax.broadcasted_iota(jnp.int32, (BAND, LANES), 1)
    lo = lane < HEAD_DIM
    mask_first = mask_ref[jnp.where(i_tile == 0, 0, 1)]
    mask_rest = mask_ref[1]

    for rr in range(n_seq):
        for bb in range(tq // BAND):
            r0 = bb * BAND
            mask = mask_first if bb == 0 else mask_rest
            lse_acc = jnp.zeros((BAND, LANES), F32)
            for j in range(n_kvh):
                kcat = kbuf[rr, r0:r0 + 2 * BAND, j * LANES:(j + 1) * LANES]
                vcat = vbuf[rr, r0:r0 + 2 * BAND, j * LANES:(j + 1) * LANES]
                k_e, k_o = kcat * sel_ref[0], kcat * sel_ref[1]
                v_e = jnp.concatenate([vcat * sel_ref[0], ones_ref[0]], axis=1)
                v_o = jnp.concatenate([vcat * sel_ref[1], ones_ref[1]], axis=1)
                qp = jnp.concatenate(
                    [q_ref[rr, r0:r0 + BAND, (j * n_pair + pi) * LANES:(j * n_pair + pi + 1) * LANES]
                     for pi in range(n_pair)], axis=0)
                if has_sink:
                    qp = jnp.concatenate([qp, qaux_ref[j * n_pair * BAND:(j + 1) * n_pair * BAND, :]], axis=1)
                    k_e = jnp.concatenate([k_e, kaux_ref[0]], axis=1)
                    k_o = jnp.concatenate([k_o, kaux_ref[1]], axis=1)
                kbd = jnp.concatenate([k_e, k_o], axis=0)
                vbd = jnp.concatenate([v_e, v_o], axis=0)
                s = lax.dot_general(qp, kbd, NT_DIMS, preferred_element_type=F32)
                ps, ms = [], []
                for pi in range(n_pair):
                    sp = s[pi * BAND:(pi + 1) * BAND] + mask
                    m_e = jnp.max(sp[:, 0:2 * BAND], axis=-1, keepdims=True)
                    m_o = jnp.max(sp[:, 2 * BAND:], axis=-1, keepdims=True)
                    m_full = jnp.concatenate([jnp.broadcast_to(m_e, (BAND, 2 * BAND)),
                                              jnp.broadcast_to(m_o, (BAND, 2 * BAND))], axis=1)
                    ms.append((m_e, m_o))
                    ps.append(jnp.exp2(sp - m_full).astype(BF16))
                p = jnp.concatenate(ps, axis=0)
                oext = jnp.dot(p, vbd, preferred_element_type=F32)
                for pi in range(n_pair):
                    oe = oext[pi * BAND:(pi + 1) * BAND]
                    o_slab = oe[:, 0:LANES] / oe[:, LANES:]
                    pair = j * n_pair + pi
                    col = pair * LANES
                    if has_sink:
                        zf = z_ref[rr, r0:r0 + BAND, col:col + LANES].astype(F32)
                        y_ref[rr, r0:r0 + BAND, col:col + LANES] = (o_slab * _silu(zf)).astype(BF16)
                    else:
                        o_ref[rr, r0:r0 + BAND, col:col + LANES] = o_slab.astype(BF16)
                        lse_slab = jnp.where(lo, ms[pi][0], ms[pi][1]) + jnp.log2(oe[:, LANES:])
                        a, b = pair * LSE_LANES, HEAD_DIM + pair * LSE_LANES
                        mine = ((lane >= a) & (lane < a + LSE_LANES)) | ((lane >= b) & (lane < b + LSE_LANES))
                        lse_acc = jnp.where(mine, lse_slab, lse_acc)
            if not has_sink:
                lse_ref[rr, r0:r0 + BAND, :] = lse_acc


def _band_attention(q, kx, vx, *, max_dist, sink=None, z=None, name):
    n_b, n_seq, length, _ = q.shape
    kw = kx.shape[3]
    n_kvh = kw // LANES
    tq = min(BAND_TILE, length)
    per_step = min(n_seq, BAND_TILE // tq)
    bpt = tq // BAND
    cur = lambda b, r, i: (b, r, i, 0)
    prev = lambda b, r, i: (b, r, jnp.maximum(i * bpt - 1, 0), 0)
    consts = _band_consts(max_dist, sink)
    const_specs = [pl.BlockSpec(c.shape, lambda b, r, i, n=c.ndim: (0,) * n) for c in consts]
    wide = pl.BlockSpec((None, per_step, tq, D_MODEL), cur)
    in_specs = const_specs + [
        wide,
        pl.BlockSpec((None, per_step, tq, kw), cur), pl.BlockSpec((None, per_step, BAND, kw), prev),
        pl.BlockSpec((None, per_step, tq, kw), cur), pl.BlockSpec((None, per_step, BAND, kw), prev)]
    args = [*consts, q, kx, kx, vx, vx]
    shape = (n_b, n_seq, length, D_MODEL)
    if sink is not None:
        in_specs.append(wide)
        args.append(z)
        out_specs, out_shape = wide, jax.ShapeDtypeStruct(shape, BF16)
    else:
        out_specs = [wide, pl.BlockSpec((None, per_step, tq, LANES), cur)]
        out_shape = [jax.ShapeDtypeStruct(shape, BF16),
                     jax.ShapeDtypeStruct((n_b, n_seq, length, LANES), F32)]
    rows = per_step * tq
    vmem = (2 * rows * D_MODEL * 2 * 3 + 2 * rows * LANES * 4 + 6 * per_step * (tq + BAND) * kw * 2 * 2
            + 2 * sum(int(np.prod(c.shape)) * c.dtype.itemsize for c in consts) + 16 * 4 * BAND * 4 * BAND * 4)
    return pl.pallas_call(
        functools.partial(_band_kernel, n_kvh=n_kvh, has_sink=sink is not None),
        grid=(n_b, n_seq // per_step, length // tq),
        in_specs=in_specs, out_specs=out_specs, out_shape=out_shape,
        scratch_shapes=[pltpu.VMEM((per_step, tq + BAND, kw), BF16), pltpu.VMEM((per_step, tq + BAND, kw), BF16)],
        compiler_params=_params(vmem, 3),
        name=name,
    )(*args)


def _dilated_merge_kernel(*refs, dils):
    n_grp = len(dils)
    it = iter(refs)
    o_refs = [next(it) for _ in range(n_grp)]
    l_refs = [next(it) for _ in range(n_grp)]
    p_refs = [next(it) if d > 1 else None for d in dils]
    e_ref, z_ref, y_ref = next(it), next(it), next(it)
    tc = MERGE_CHUNK

    def split3(a):
        hi = a.astype(BF16)
        r1 = a - hi.astype(F32)
        mid = r1.astype(BF16)
        return hi, mid, (r1 - mid.astype(F32)).astype(BF16)

    for c in range(z_ref.shape[0] // tc):
        outs, lses = [], []
        for d, o_ref, l_ref, p_ref in zip(dils, o_refs, l_refs, p_refs):
            fold = slice(c * tc // d, (c + 1) * tc // d)
            o = o_ref[:, fold, :].reshape(tc, D_MODEL)
            lse = l_ref[:, fold, :].reshape(tc, LANES)
            if d > 1:
                pt = p_ref[...]
                o = jnp.dot(pt, o, preferred_element_type=F32)
                parts = jnp.concatenate(split3(lse), axis=1)
                lp = jnp.dot(pt, parts, preferred_element_type=F32)
                lse = lp[:, 0:LANES] + lp[:, LANES:2 * LANES] + lp[:, 2 * LANES:]
            else:
                o = o.astype(F32)
            outs.append(o)
            lses.append(lse)
        m = functools.reduce(jnp.maximum, lses)
        es = [jnp.exp2(l - m) for l in lses]
        den = functools.reduce(lambda a, b: a + b, es)
        acc = jnp.zeros((tc, D_MODEL), F32)
        for e, o in zip(es, outs):
            hi, mid, _ = split3(e / den)
            w = jnp.dot(jnp.concatenate([hi, mid], axis=1), e_ref[...], preferred_element_type=F32)
            acc = acc + w * o
        rows = slice(c * tc, (c + 1) * tc)
        y_ref[rows, :] = (acc * _silu(z_ref[rows, :].astype(F32))).astype(BF16)


def _dilated_merge(outs, lses, z2, *, n_b, seq, dils):
    tm = 2 * MERGE_CHUNK
    seq_tiles = seq // tm
    expand = np.zeros((LANES, D_MODEL), np.float32)
    for h in range(N_HEADS):
        expand[(h % 2) * HEAD_DIM + (h // 2) * LSE_LANES, h * HEAD_DIM:(h + 1) * HEAD_DIM] = 1.0
    expand = jnp.asarray(np.concatenate([expand, expand], axis=0), BF16)
    omap = lambda i: (i // seq_tiles, 0, i % seq_tiles, 0)
    in_specs = [pl.BlockSpec((None, d, tm // d, D_MODEL), omap) for d in dils]
    in_specs += [pl.BlockSpec((None, d, tm // d, LANES), omap) for d in dils]
    perms = [jnp.asarray(_fold_perm(MERGE_CHUNK, d).T, BF16) for d in dils if d > 1]
    in_specs += [pl.BlockSpec(p.shape, lambda i: (0, 0)) for p in perms]
    in_specs += [pl.BlockSpec(expand.shape, lambda i: (0, 0)), pl.BlockSpec((tm, D_MODEL), lambda i: (i, 0))]
    vmem = (2 * tm * D_MODEL * (3 * 2 + 2 + 2) + 2 * 3 * tm * LANES * 4 + 4 * tm * tm * 2
            + 2 * 2 * LANES * D_MODEL * 2 + 10 * tm * D_MODEL * 4)
    return pl.pallas_call(
        functools.partial(_dilated_merge_kernel, dils=tuple(dils)),
        grid=(n_b * seq // tm,),
        in_specs=in_specs,
        out_specs=pl.BlockSpec((tm, D_MODEL), lambda i: (i, 0)),
        out_shape=jax.ShapeDtypeStruct((n_b * seq, D_MODEL), BF16),
        compiler_params=_params(vmem, 1),
        name="dilated_merge",
    )(*outs, *lses, *perms, expand, z2)


def _moba_gate_kernel(q_ref, kx_ref, qext_ref, km_scr, *, n_blk):
    t = pl.program_id(1)
    n_kvh = qext_ref.shape[0]
    g = N_HEADS // n_kvh
    tq = MOBA_BLOCK
    rows_n = g * tq
    lane = lax.broadcasted_iota(jnp.int32, (tq, LANES), 1)
    lo = lane < HEAD_DIM

    @pl.when(t == 0)
    def _block_means():
        for hh in range(n_kvh):
            means = []
            for n in range(n_blk):
                kb = kx_ref[n * MOBA_BLOCK:(n + 1) * MOBA_BLOCK, hh * LANES:(hh + 1) * LANES].astype(F32)
                means.append(jnp.sum(kb, axis=0, keepdims=True) * (1.0 / MOBA_BLOCK))
            km = jnp.concatenate(means, axis=0)
            km = jnp.where(lax.broadcasted_iota(jnp.int32, km.shape, 1) < HEAD_DIM, km, 0.0)
            km_scr[hh] = jnp.concatenate(
                [jnp.zeros((HEAD_DIM, LANES), F32), km,
                 jnp.zeros((LANES - HEAD_DIM - n_blk, LANES), F32)], axis=0).astype(BF16)

    for hh in range(n_kvh):
        qrows = jnp.concatenate(
            [q_ref[:, (hh * g + gi) * LANES:(hh * g + gi + 1) * LANES] for gi in range(g)], axis=0)

        gate_t = lax.dot_general(km_scr[hh], qrows, NT_DIMS, preferred_element_type=F32)
        gate = gate_t[HEAD_DIM:HEAD_DIM + n_blk]
        blk = lax.broadcasted_iota(jnp.int32, gate.shape, 0).astype(F32)
        tf = t.astype(F32)
        gate = jnp.where(blk < tf, gate, -jnp.inf)
        sel = jnp.zeros(gate.shape, F32)
        for _ in range(MOBA_TOPK):
            mx = jnp.max(gate, axis=0, keepdims=True)
            is_max = (gate == mx) & (mx > -jnp.inf)
            first = jnp.min(jnp.where(is_max, blk, float(n_blk)), axis=0, keepdims=True)
            pick = blk == first
            sel = jnp.where(pick, 1.0, sel)
            gate = jnp.where(pick, -jnp.inf, gate)
        bias_t = jnp.where((sel > 0.0) | (blk == tf), 0.0, NEG_BIG)
        bias_full = jnp.concatenate(
            [jnp.zeros((HEAD_DIM, rows_n), F32), bias_t,
             jnp.zeros((LANES - HEAD_DIM - n_blk, rows_n), F32)], axis=0)
        qext_ref[hh] = qrows + bias_full.T.astype(BF16)


def _moba_gate(q, kx):
    n_b, seq, qw = q.shape
    n_kvh = kx.shape[2] // LANES
    g = N_HEADS // n_kvh
    tq = MOBA_BLOCK
    vmem = (2 * tq * qw * 2 + 2 * seq * n_kvh * LANES * 2 + 2 * n_kvh * g * tq * LANES * 2
            + n_kvh * 8 * g * tq * LANES * 4)
    return pl.pallas_call(
        functools.partial(_moba_gate_kernel, n_blk=seq // MOBA_BLOCK),
        grid=(n_b, seq // tq),
        in_specs=[pl.BlockSpec((None, tq, qw), lambda b, t: (b, t, 0)),
                  pl.BlockSpec((None, seq, n_kvh * LANES), lambda b, t: (b, 0, 0))],
        out_specs=pl.BlockSpec((None, n_kvh, g * tq, LANES), lambda b, t: (b, 0, t, 0)),
        out_shape=jax.ShapeDtypeStruct((n_b, n_kvh, seq * g, LANES), BF16),
        scratch_shapes=[pltpu.VMEM((n_kvh, LANES, LANES), BF16)],
        compiler_params=_params(vmem, 2),
        name="moba_gate",
    )(q, kx)


def _moba_kernel(tril_ref, qext_ref, kx_ref, vx_ref, z_ref, y_ref, sa_scr, sb_scr, m_scr, acc_scr):
    t = pl.program_id(2)
    g = 4
    tq = MOBA_BLOCK
    rows_n = g * tq
    lane = lax.broadcasted_iota(jnp.int32, (tq, LANES), 1)
    lo = lane < HEAD_DIM

    def block(ref, blk_idx):
        r0 = pl.multiple_of(blk_idx * MOBA_BLOCK, MOBA_BLOCK)
        return ref[pl.ds(r0, MOBA_BLOCK), :]

    def scores(blk_idx):
        return lax.dot_general(qext_ref[...], block(kx_ref, blk_idx), NT_DIMS, preferred_element_type=F32)

    def blocks_of(j):
        first = jnp.where(j == 0, t, 2 * j - 1)
        second = jnp.where(2 * j + 1 <= t, 2 * j, t + 1)
        return first, second

    def produce(dst, j):
        first, second = blocks_of(j)
        dst[:, 0:tq] = scores(first)
        dst[:, tq:2 * tq] = scores(second)

    def consume(src, j):
        first, second = blocks_of(j)
        sn = src[...]
        m_old = m_scr[...]
        m_new = jnp.maximum(m_old, jnp.max(sn, axis=-1, keepdims=True))
        alpha = jnp.exp2(m_old - m_new)
        pn = jnp.exp2(sn - jnp.concatenate([m_new] * (2 * tq // LANES), axis=1)).astype(BF16)
        acc_scr[...] = (alpha * acc_scr[...]
                        + jnp.dot(pn[:, 0:tq], block(vx_ref, first), preferred_element_type=F32)
                        + jnp.dot(pn[:, tq:2 * tq], block(vx_ref, second), preferred_element_type=F32))
        m_scr[...] = m_new

    n_items = (t + 2) // 2
    tril = tril_ref[...]
    s0 = scores(t)
    sa_scr[:, 0:tq] = jnp.concatenate([s0[gi * tq:(gi + 1) * tq] + tril for gi in range(g)], axis=0)
    sa_scr[:, tq:2 * tq] = scores(blocks_of(0)[1])
    m_scr[...] = jnp.full((rows_n, LANES), NEG_BIG, F32)
    acc_scr[...] = jnp.zeros((rows_n, LANES), F32)

    def stage(src, dst, j):
        if dst is not None:
            produce(dst, jnp.minimum(j + 1, n_items - 1))
        consume(src, j)

    def pair(pp, carry):
        stage(sa_scr, sb_scr, 2 * pp)
        stage(sb_scr, sa_scr, 2 * pp + 1)
        return carry

    lax.fori_loop(0, n_items // 2, pair, 0)
    pl.when(n_items % 2 == 1)(functools.partial(stage, sa_scr, None, n_items - 1))

    acc = acc_scr[...]
    for pi in range(g // 2):
        oe = acc[(2 * pi) * tq:(2 * pi + 1) * tq]
        oo = acc[(2 * pi + 1) * tq:(2 * pi + 2) * tq]
        o_slab = jnp.where(lo, oe / pltpu.roll(oe, HEAD_DIM, 1), pltpu.roll(oo, HEAD_DIM, 1) / oo)
        zf = z_ref[:, pi * LANES:(pi + 1) * LANES].astype(F32)
        y_ref[:, pi * LANES:(pi + 1) * LANES] = (o_slab * _silu(zf)).astype(BF16)


def _moba_attention(qext, kx, vx, z):
    n_b, seq, _ = z.shape
    n_kvh = kx.shape[2] // LANES
    g = N_HEADS // n_kvh
    tq = MOBA_BLOCK
    qw = g * HEAD_DIM
    tril = jnp.asarray(np.where(np.tril(np.ones((tq, tq), bool)), 0.0, NEG_BIG), F32)
    tile = lambda b, j, t: (b, t, j)
    whole = lambda b, j, t: (b, 0, j)
    vmem = (2 * 2 * seq * LANES * 2 + 2 * 2 * tq * qw * 2 + tq * tq * 4 * 2
            + g * tq * LANES * (2 * 2 + 4 + 4) + 2 * g * tq * 2 * tq * 4 + 6 * g * tq * 2 * tq * 4)
    return pl.pallas_call(
        _moba_kernel,
        grid=(n_b, n_kvh, seq // tq),
        in_specs=[pl.BlockSpec((tq, tq), lambda b, j, t: (0, 0)),
                  pl.BlockSpec((None, None, g * tq, LANES), lambda b, j, t: (b, j, t, 0)),
                  pl.BlockSpec((None, seq, LANES), whole), pl.BlockSpec((None, seq, LANES), whole),
                  pl.BlockSpec((None, tq, qw), tile)],
        out_specs=pl.BlockSpec((None, tq, qw), tile),
        out_shape=jax.ShapeDtypeStruct((n_b, seq, D_MODEL), BF16),
        scratch_shapes=[pltpu.VMEM((g * tq, 2 * tq), F32), pltpu.VMEM((g * tq, 2 * tq), F32),
                        pltpu.VMEM((g * tq, LANES), F32), pltpu.VMEM((g * tq, LANES), F32)],
        compiler_params=_params(vmem, 3),
        name="moba_attention",
    )(tril, qext, kx, vx, z)


def _rope_base(seq):
    half = ROT_DIM // 2
    inv = ROPE_THETA ** (-jnp.arange(0, ROT_DIM, 2, dtype=F32) / ROT_DIM)
    c = np.arange(LANES) % HEAD_DIM
    rot = c < ROT_DIM
    freq = jnp.where(rot, inv[c % half], 0.0)
    ang = jnp.arange(seq, dtype=F32)[:, None] * freq[None, :]
    cos, sin = jnp.cos(ang), jnp.sin(ang)
    first = jnp.asarray(c < half, F32)
    second = jnp.asarray(rot & (c >= half), F32)
    return jnp.stack([cos, -sin * first, sin * second])


def _rope_table(tab, dil):
    if dil == 1:
        return tab
    seq = tab.shape[1]
    return tab.reshape(3, seq // TM, TM // dil, dil, LANES).transpose(0, 1, 3, 2, 4).reshape(3, seq, LANES)


def _prep_w_in(w_in, kind):
    kv = KV_HEADS[kind] * HEAD_DIM
    n_grp = len(DILATED_GROUPS) if kind == 1 else 1
    scale = np.ones((w_in.shape[1],), np.float32)
    for g in range(n_grp):
        off = g * (D_MODEL + 2 * kv)
        scale[off:off + D_MODEL] = Q_SCALE
    return (w_in * jnp.asarray(scale)).astype(BF16)


def kernel(x, w_in_0, sink_0, w_out_0, ln_g_0, ln_b_0, w_in_1, w_out_1, ln_g_1, ln_b_1, w_in_2, w_out_2, ln_g_2, ln_b_2, w_in_3, sink_3, w_out_3, ln_g_3, ln_b_3):
    n_b, seq, d_model = x.shape
    assert d_model == D_MODEL and seq % max(dl * BAND for _, dl in DILATED_GROUPS) == 0 and seq % TM == 0
    assert (seq // MOBA_BLOCK) % 8 == 0 and seq // MOBA_BLOCK <= LANES - HEAD_DIM
    layers = [(w_in_0, sink_0, w_out_0, ln_g_0, ln_b_0),
              (w_in_1, None, w_out_1, ln_g_1, ln_b_1),
              (w_in_2, None, w_out_2, ln_g_2, ln_b_2),
              (w_in_3, sink_3, w_out_3, ln_g_3, ln_b_3)]
    dil_b = [dl for _, dl in DILATED_GROUPS]
    rope_nat = _rope_base(seq)
    x2 = x.reshape(n_b * seq, D_MODEL)
    for i, (w_in, sink, w_out, ln_g, ln_b) in enumerate(layers):
        kind = i % N_MIXERS
        w = _prep_w_in(w_in, kind)
        nat = lambda a: a.reshape(n_b, seq, a.shape[-1])
        if kind == 1:
            tabs = [_rope_table(rope_nat, dl) for dl in dil_b]
            grp, z = _inproj(x2, tabs, w, n_b=n_b, seq=seq, dils=dil_b, n_kvh=KV_HEADS[kind], moba=False)
            outs, lses = [], []
            for gi, ((window, dl), (q, kx, vx)) in enumerate(zip(DILATED_GROUPS, grp)):
                o, lse = _band_attention(q, kx, vx, max_dist=window // dl, name="dilated_group_%d" % gi)
                outs.append(o)
                lses.append(lse)
            y = _dilated_merge(outs, lses, z, n_b=n_b, seq=seq, dils=dil_b)
        else:
            grp, z = _inproj(x2, [rope_nat], w, n_b=n_b, seq=seq, dils=[1], n_kvh=KV_HEADS[kind],
                             moba=(kind == 2))
            q, kx, vx = grp[0]
            if kind == 0:
                y = _band_attention(q, kx, vx, max_dist=WINDOW_A - 1, sink=sink,
                                    z=z.reshape(n_b, 1, seq, D_MODEL), name="swa_attention")
            else:
                y = _moba_attention(_moba_gate(nat(q), nat(kx)), nat(kx), nat(vx), nat(z))
        x2 = _outproj(y.reshape(n_b * seq, D_MODEL), x2, w_out.astype(BF16),
                      ln_g.reshape(1, D_MODEL).astype(F32), ln_b.reshape(1, D_MODEL).astype(F32))
    return x2.reshape(n_b, seq, D_MODEL)
```

```python
import functools
import math

import numpy as np
import jax
import jax.numpy as jnp
from jax import lax
from jax.experimental import pallas as pl
from jax.experimental.pallas import tpu as pltpu

D_MODEL = 1024
HEAD_DIM = 64
N_HEADS = D_MODEL // HEAD_DIM
ROT_DIM = HEAD_DIM // 4
ROPE_THETA = 500000.0
DEPTH = 4
N_MIXERS = 3
KV_HEADS = (2, 4, 4)
WINDOW_A = 128
DILATED_GROUPS = ((128, 1), (512, 4), (2048, 16))
BAND = 128
MOBA_BLOCK = 256
MOBA_TOPK = 3
DEEPNORM_ALPHA = (2 * DEPTH) ** 0.25
LN_EPS = 1e-5
LOG2E = math.log2(math.e)
Q_SCALE = HEAD_DIM ** -0.5 * LOG2E

LANES = 128
LSE_LANES = LANES // N_HEADS
NEG_BIG = -1e30
VMEM_CAP = 60000 * 1024
TM = 512
MERGE_CHUNK = 256
BAND_TILE = 1024
MOBA_UNROLL = 4

F32 = jnp.float32
BF16 = jnp.bfloat16
NT_DIMS = (((1,), (1,)), ((), ()))


def _params(vmem_bytes, n_grid):
    limit = int(min(max(2 * vmem_bytes, 32 * 1024 * 1024), VMEM_CAP))
    return pltpu.CompilerParams(dimension_semantics=("arbitrary",) * n_grid,
                                vmem_limit_bytes=limit)


def _silu(z):
    return z / (1.0 + jnp.exp(-z))


def _fold_perm(n, dil):
    p = np.zeros((n, n), np.float32)
    idx = np.arange(n)
    p[(idx % dil) * (n // dil) + idx // dil, idx] = 1.0
    return p


def _moba_bias(km, qrows, t, n_blk):
    rows_n = qrows.shape[0]
    gate_t = lax.dot_general(km, qrows, NT_DIMS, preferred_element_type=F32)
    gate = gate_t[HEAD_DIM:HEAD_DIM + n_blk]
    blk = lax.broadcasted_iota(jnp.int32, gate.shape, 0).astype(F32)
    tf = t.astype(F32)
    gate = jnp.where(blk < tf, gate, -jnp.inf)
    sel = jnp.zeros(gate.shape, F32)
    for _ in range(MOBA_TOPK):
        mx = jnp.max(gate, axis=0, keepdims=True)
        is_max = (gate == mx) & (mx > -jnp.inf)
        first = jnp.min(jnp.where(is_max, blk, float(n_blk)), axis=0, keepdims=True)
        pick = blk == first
        sel = jnp.where(pick, 1.0, sel)
        gate = jnp.where(pick, -jnp.inf, gate)
    bias_t = jnp.where((sel > 0.0) | (blk == tf), 0.0, NEG_BIG)
    bias_full = jnp.concatenate(
        [jnp.zeros((HEAD_DIM, rows_n), F32), bias_t,
         jnp.zeros((LANES - HEAD_DIM - n_blk, rows_n), F32)], axis=0)
    return bias_full.T.astype(BF16)


def _inproj_kernel(*refs, dils, n_kvh, moba, seq_tiles):
    n_grp = len(dils)
    kv = n_kvh * HEAD_DIM
    it = iter(refs)
    x_ref = next(it)
    groups = [(d, next(it) if d > 1 else None, next(it)) for d in dils]
    w_ref = next(it)
    outs = [(next(it), next(it), next(it)) for _ in range(n_grp)]
    z_ref = next(it)

    def proj(lhs, col, width):
        return jnp.dot(lhs, w_ref[:, col:col + width], preferred_element_type=F32)

    tm = x_ref.shape[0]
    xb = x_ref[...].astype(BF16)
    lane = lax.broadcasted_iota(jnp.int32, (tm, LANES), 1)
    lo = lane < HEAD_DIM
    if moba:
        km_scr = next(it)
        tile = pl.program_id(0) % seq_tiles
        n_q = tm // MOBA_BLOCK
        n_blk = seq_tiles * n_q
        row = lax.broadcasted_iota(jnp.int32, (tm, LANES), 0)
        pos = tile * tm + row
        k_aux = jnp.where(lane == HEAD_DIM + lax.shift_right_logical(pos, 8), 1.0, 0.0)

        @pl.when(tile == 0)
        def _new_sequence():
            km_scr[...] = jnp.zeros(km_scr.shape, F32)

    off = 0
    for (d, perm_ref, rope_ref), (q_ref, kx_ref, vx_ref) in zip(groups, outs):
        n = tm // d
        xg = xb if d == 1 else jnp.dot(perm_ref[...], xb, preferred_element_type=F32).astype(BF16)
        cs, sa, sb = rope_ref[0], rope_ref[1], rope_ref[2]

        def rope(t):
            return (t * cs + pltpu.roll(t, LANES - ROT_DIM // 2, 1) * sa
                    + pltpu.roll(t, ROT_DIM // 2, 1) * sb)

        def put(ref, col, val):
            for r in range(d):
                ref[r, :, col:col + val.shape[1]] = val[r * n:(r + 1) * n]

        acc_k = proj(xg, off + D_MODEL, kv)
        acc_v = proj(xg, off + D_MODEL + kv, kv)
        for c in range(n_kvh // 2):
            k2 = rope(acc_k[:, c * LANES:(c + 1) * LANES])
            v2 = acc_v[:, c * LANES:(c + 1) * LANES]
            k2r = pltpu.roll(k2, HEAD_DIM, 1)
            v2r = pltpu.roll(v2, HEAD_DIM, 1)
            for e in range(2):
                j = 2 * c + e
                if moba:
                    ke = jnp.where(lo, k2 if e == 0 else k2r, 0.0)
                    for bq in range(n_q):
                        mean = jnp.sum(ke[bq * MOBA_BLOCK:(bq + 1) * MOBA_BLOCK], axis=0,
                                       keepdims=True) * (1.0 / MOBA_BLOCK)
                        km_scr[j, pl.ds(HEAD_DIM + tile * n_q + bq, 1), :] = mean
                    kk = jnp.where(lo, ke, k_aux)
                    vv = jnp.where(lo, v2 if e == 0 else v2r, 1.0)
                else:
                    kk = jnp.where(lo, k2, k2r) if e == 0 else jnp.where(lo, k2r, k2)
                    vv = jnp.where(lo, v2, v2r) if e == 0 else jnp.where(lo, v2r, v2)
                put(kx_ref, j * LANES, kk.astype(BF16))
                put(vx_ref, j * LANES, vv.astype(BF16))

        for c in range(D_MODEL // 512):
            acc = proj(xg, off + c * 512, 512)
            heads = []
            for s in range(4):
                q2 = rope(acc[:, s * LANES:(s + 1) * LANES])
                if moba:
                    heads.append(jnp.where(lo, q2, 0.0).astype(BF16))
                    heads.append(jnp.where(lo, pltpu.roll(q2, HEAD_DIM, 1), 0.0).astype(BF16))
                else:
                    put(q_ref, c * 512 + s * LANES, q2.astype(BF16))
            if moba:
                g = N_HEADS // n_kvh
                for e in range(len(heads) // g):
                    j = c * (len(heads) // g) + e
                    km = km_scr[j].astype(BF16)
                    for bq in range(n_q):
                        rows = slice(bq * MOBA_BLOCK, (bq + 1) * MOBA_BLOCK)
                        qrows = jnp.concatenate([heads[e * g + gi][rows] for gi in range(g)], axis=0)
                        bias = _moba_bias(km, qrows, tile * n_q + bq, n_blk)
                        q_ref[j, bq * g * MOBA_BLOCK:(bq + 1) * g * MOBA_BLOCK, :] = qrows + bias
        off += D_MODEL + 2 * kv

    for c in range(D_MODEL // 512):
        z_ref[:, c * 512:(c + 1) * 512] = proj(xb, off + c * 512, 512).astype(BF16)


def _inproj(x2, rope_tabs, w, *, n_b, seq, dils, n_kvh, moba):
    tm = TM
    seq_tiles = seq // tm
    kw = n_kvh * LANES
    const = lambda a: pl.BlockSpec(a.shape, lambda i: (0, 0), pipeline_mode=pl.Buffered(1))
    in_specs = [pl.BlockSpec((tm, D_MODEL), lambda i: (i, 0))]
    args = [x2]
    for d, tab in zip(dils, rope_tabs):
        if d > 1:
            perm = jnp.asarray(_fold_perm(tm, d), BF16)
            in_specs.append(const(perm))
            args.append(perm)
        in_specs.append(pl.BlockSpec((3, tm, LANES), lambda i: (0, i % seq_tiles, 0)))
        args.append(tab)
    in_specs.append(const(w))
    args.append(w)
    out_specs, out_shape = [], []
    g = N_HEADS // n_kvh
    for d in dils:
        omap = lambda i: (i // seq_tiles, 0, i % seq_tiles, 0)
        for width in (D_MODEL, kw, kw):
            out_specs.append(pl.BlockSpec((None, d, tm // d, width), omap))
            out_shape.append(jax.ShapeDtypeStruct((n_b, d, seq // d, width), BF16))
    if moba:
        out_specs[0] = pl.BlockSpec((None, n_kvh, tm * g, LANES), omap)
        out_shape[0] = jax.ShapeDtypeStruct((n_b, n_kvh, seq * g, LANES), BF16)
    out_specs.append(pl.BlockSpec((tm, D_MODEL), lambda i: (i, 0)))
    out_shape.append(jax.ShapeDtypeStruct((n_b * seq, D_MODEL), BF16))
    scratch = [pltpu.VMEM((n_kvh, LANES, LANES), F32)] if moba else []
    vmem = (2 * tm * D_MODEL * 4 + len(dils) * (2 * 3 * tm * LANES * 4 + tm * tm * 2)
            + 2 * D_MODEL * w.shape[1] + 2 * 2 * tm * (len(dils) * (2 * D_MODEL + 2 * kw) + D_MODEL)
            + 6 * tm * 512 * 4)
    res = pl.pallas_call(
        functools.partial(_inproj_kernel, dils=tuple(dils), n_kvh=n_kvh, moba=moba, seq_tiles=seq_tiles),
        grid=(n_b * seq // tm,),
        in_specs=in_specs, out_specs=out_specs, out_shape=out_shape, scratch_shapes=scratch,
        compiler_params=_params(vmem, 1),
        name="inproj",
    )(*args)
    return [tuple(res[3 * g:3 * g + 3]) for g in range(len(dils))], res[-1]


def _outproj_kernel(y_ref, x_ref, w_ref, g_ref, b_ref, o_ref):
    rows = 256
    for c in range(y_ref.shape[0] // rows):
        sl = slice(c * rows, (c + 1) * rows)
        t = DEEPNORM_ALPHA * x_ref[sl, :] + jnp.dot(y_ref[sl, :], w_ref[...], preferred_element_type=F32)
        mu = jnp.mean(t, axis=-1, keepdims=True)
        d = t - mu
        var = jnp.mean(d * d, axis=-1, keepdims=True)
        o_ref[sl, :] = d * lax.rsqrt(var + LN_EPS) * g_ref[...] + b_ref[...]


def _outproj(y2, x2, w, g, b):
    t_rows = x2.shape[0]
    tm = 2 * TM
    row = lambda i: (i, 0)
    full = lambda i: (0, 0)
    vmem = 2 * tm * D_MODEL * (2 + 4 + 4) + 2 * D_MODEL * D_MODEL * 2 + 3 * 128 * D_MODEL * 4
    return pl.pallas_call(
        _outproj_kernel,
        grid=(t_rows // tm,),
        in_specs=[pl.BlockSpec((tm, D_MODEL), row), pl.BlockSpec((tm, D_MODEL), row),
                  pl.BlockSpec((D_MODEL, D_MODEL), full),
                  pl.BlockSpec((1, D_MODEL), full), pl.BlockSpec((1, D_MODEL), full)],
        out_specs=pl.BlockSpec((tm, D_MODEL), row),
        out_shape=jax.ShapeDtypeStruct((t_rows, D_MODEL), F32),
        compiler_params=_params(vmem, 1),
        name="outproj_ln",
    )(y2, x2, w, g, b)


def _band_consts(max_dist, sink):
    i = np.arange(BAND)[:, None]
    c = np.arange(2 * BAND)[None, :]
    dist = BAND + i - c
    ok = (dist >= 0) & (dist <= max_dist)
    first = ok & (c >= BAND)
    sel = np.zeros((2, 2 * BAND, LANES), np.float32)
    sel[0, :, :HEAD_DIM] = 1.0
    sel[1, :, HEAD_DIM:] = 1.0
    ones = sel.copy()
    if sink is not None:
        assert not ok[:, 0].any()
        ok[:, 0] = True
        first[:, 0] = True
        sel[:, 0, :] = 0.0
    mask = np.where(np.stack([first, ok]), 0.0, NEG_BIG)
    consts = [jnp.asarray(np.concatenate([mask, mask], axis=2), F32),
              jnp.asarray(sel, BF16), jnp.asarray(ones, BF16)]
    if sink is not None:
        s2 = sink.astype(F32) * LOG2E
        hi = s2.astype(BF16)
        lo = (s2 - hi.astype(F32)).astype(BF16)
        per_pair = jnp.stack([hi[0::2], lo[0::2], hi[1::2], lo[1::2]], axis=1)
        per_pair = jnp.pad(per_pair, ((0, 0), (0, LANES - 4)))
        k_aux = np.zeros((2, 2 * BAND, LANES), np.float32)
        k_aux[0, 0, 0:2] = 1.0
        k_aux[1, 0, 2:4] = 1.0
        consts += [jnp.repeat(per_pair, BAND, axis=0), jnp.asarray(k_aux, BF16)]
    return consts


def _band_kernel(*refs, n_kvh, has_sink):
    if has_sink:
        (mask_ref, sel_ref, ones_ref, qaux_ref, kaux_ref, q_ref, kc_ref, kp_ref, vc_ref, vp_ref, z_ref,
         y_ref, kbuf, vbuf) = refs
    else:
        mask_ref, sel_ref, ones_ref, q_ref, kc_ref, kp_ref, vc_ref, vp_ref, o_ref, lse_ref, kbuf, vbuf = refs
    n_pair = N_HEADS // n_kvh // 2
    n_seq, tq = q_ref.shape[0], q_ref.shape[1]
    i_tile = pl.program_id(2)
    kbuf[:, 0:BAND, :] = kp_ref[...]
    kbuf[:, BAND:BAND + tq, :] = kc_ref[...]
    vbuf[:, 0:BAND, :] = vp_ref[...]
    vbuf[:, BAND:BAND + tq, :] = vc_ref[...]
    lane = lax.broadcasted_iota(jnp.int32, (BAND, LANES), 1)
    lo = lane < HEAD_DIM
    mask_first = mask_ref[jnp.where(i_tile == 0, 0, 1)]
    mask_rest = mask_ref[1]

    for rr in range(n_seq):
        for bb in range(tq // BAND):
            r0 = bb * BAND
            mask = mask_first if bb == 0 else mask_rest
            lse_acc = jnp.zeros((BAND, LANES), F32)
            for j in range(n_kvh):
                kcat = kbuf[rr, r0:r0 + 2 * BAND, j * LANES:(j + 1) * LANES]
                vcat = vbuf[rr, r0:r0 + 2 * BAND, j * LANES:(j + 1) * LANES]
                k_e, k_o = kcat * sel_ref[0], kcat * sel_ref[1]
                v_e = jnp.concatenate([vcat * sel_ref[0], ones_ref[0]], axis=1)
                v_o = jnp.concatenate([vcat * sel_ref[1], ones_ref[1]], axis=1)
                qp = jnp.concatenate(
                    [q_ref[rr, r0:r0 + BAND, (j * n_pair + pi) * LANES:(j * n_pair + pi + 1) * LANES]
                     for pi in range(n_pair)], axis=0)
                if has_sink:
                    qp = jnp.concatenate([qp, qaux_ref[j * n_pair * BAND:(j + 1) * n_pair * BAND, :]], axis=1)
                    k_e = jnp.concatenate([k_e, kaux_ref[0]], axis=1)
                    k_o = jnp.concatenate([k_o, kaux_ref[1]], axis=1)
                kbd = jnp.concatenate([k_e, k_o], axis=0)
                vbd = jnp.concatenate([v_e, v_o], axis=0)
                s = lax.dot_general(qp, kbd, NT_DIMS, preferred_element_type=F32)
                ps, ms = [], []
                for pi in range(n_pair):
                    sp = s[pi * BAND:(pi + 1) * BAND] + mask
                    m_e = jnp.max(sp[:, 0:2 * BAND], axis=-1, keepdims=True)
                    m_o = jnp.max(sp[:, 2 * BAND:], axis=-1, keepdims=True)
                    m_full = jnp.concatenate([jnp.broadcast_to(m_e, (BAND, 2 * BAND)),
                                              jnp.broadcast_to(m_o, (BAND, 2 * BAND))], axis=1)
                    ms.append((m_e, m_o))
                    ps.append(jnp.exp2(sp - m_full).astype(BF16))
                p = jnp.concatenate(ps, axis=0)
                oext = jnp.dot(p, vbd, preferred_element_type=F32)
                for pi in range(n_pair):
                    oe = oext[pi * BAND:(pi + 1) * BAND]
                    o_slab = oe[:, 0:LANES] / oe[:, LANES:]
                    pair = j * n_pair + pi
                    col = pair * LANES
                    if has_sink:
                        zf = z_ref[rr, r0:r0 + BAND, col:col + LANES].astype(F32)
                        y_ref[rr, r0:r0 + BAND, col:col + LANES] = (o_slab * _silu(zf)).astype(BF16)
                    else:
                        o_ref[rr, r0:r0 + BAND, col:col + LANES] = o_slab.astype(BF16)
                        lse_slab = jnp.where(lo, ms[pi][0], ms[pi][1]) + jnp.log2(oe[:, LANES:])
                        a, b = pair * LSE_LANES, HEAD_DIM + pair * LSE_LANES
                        mine = ((lane >= a) & (lane < a + LSE_LANES)) | ((lane >= b) & (lane < b + LSE_LANES))
                        lse_acc = jnp.where(mine, lse_slab, lse_acc)
            if not has_sink:
                lse_ref[rr, r0:r0 + BAND, :] = lse_acc


def _band_attention(q, kx, vx, *, max_dist, sink=None, z=None, name):
    n_b, n_seq, length, _ = q.shape
    kw = kx.shape[3]
    n_kvh = kw // LANES
    tq = min(BAND_TILE, length)
    per_step = min(n_seq, BAND_TILE // tq)
    bpt = tq // BAND
    cur = lambda b, r, i: (b, r, i, 0)
    prev = lambda b, r, i: (b, r, jnp.maximum(i * bpt - 1, 0), 0)
    consts = _band_consts(max_dist, sink)
    const_specs = [pl.BlockSpec(c.shape, lambda b, r, i, n=c.ndim: (0,) * n) for c in consts]
    wide = pl.BlockSpec((None, per_step, tq, D_MODEL), cur)
    in_specs = const_specs + [
        wide,
        pl.BlockSpec((None, per_step, tq, kw), cur), pl.BlockSpec((None, per_step, BAND, kw), prev),
        pl.BlockSpec((None, per_step, tq, kw), cur), pl.BlockSpec((None, per_step, BAND, kw), prev)]
    args = [*consts, q, kx, kx, vx, vx]
    shape = (n_b, n_seq, length, D_MODEL)
    if sink is not None:
        in_specs.append(wide)
        args.append(z)
        out_specs, out_shape = wide, jax.ShapeDtypeStruct(shape, BF16)
    else:
        out_specs = [wide, pl.BlockSpec((None, per_step, tq, LANES), cur)]
        out_shape = [jax.ShapeDtypeStruct(shape, BF16),
                     jax.ShapeDtypeStruct((n_b, n_seq, length, LANES), F32)]
    rows = per_step * tq
    vmem = (2 * rows * D_MODEL * 2 * 3 + 2 * rows * LANES * 4 + 6 * per_step * (tq + BAND) * kw * 2 * 2
            + 2 * sum(int(np.prod(c.shape)) * c.dtype.itemsize for c in consts) + 16 * 4 * BAND * 4 * BAND * 4)
    return pl.pallas_call(
        functools.partial(_band_kernel, n_kvh=n_kvh, has_sink=sink is not None),
        grid=(n_b, n_seq // per_step, length // tq),
        in_specs=in_specs, out_specs=out_specs, out_shape=out_shape,
        scratch_shapes=[pltpu.VMEM((per_step, tq + BAND, kw), BF16), pltpu.VMEM((per_step, tq + BAND, kw), BF16)],
        compiler_params=_params(vmem, 3),
        name=name,
    )(*args)


def _dilated_merge_kernel(*refs, dils):
    n_grp = len(dils)
    it = iter(refs)
    o_refs = [next(it) for _ in range(n_grp)]
    l_refs = [next(it) for _ in range(n_grp)]
    p_refs = [next(it) if d > 1 else None for d in dils]
    e_ref, z_ref, y_ref = next(it), next(it), next(it)
    tc = MERGE_CHUNK

    def split3(a):
        hi = a.astype(BF16)
        r1 = a - hi.astype(F32)
        mid = r1.astype(BF16)
        return hi, mid, (r1 - mid.astype(F32)).astype(BF16)

    for c in range(z_ref.shape[0] // tc):
        outs, lses = [], []
        for d, o_ref, l_ref, p_ref in zip(dils, o_refs, l_refs, p_refs):
            fold = slice(c * tc // d, (c + 1) * tc // d)
            o = o_ref[:, fold, :].reshape(tc, D_MODEL)
            lse = l_ref[:, fold, :].reshape(tc, LANES)
            if d > 1:
                pt = p_ref[...]
                o = jnp.dot(pt, o, preferred_element_type=F32)
                parts = jnp.concatenate(split3(lse), axis=1)
                lp = jnp.dot(pt, parts, preferred_element_type=F32)
                lse = lp[:, 0:LANES] + lp[:, LANES:2 * LANES] + lp[:, 2 * LANES:]
            else:
                o = o.astype(F32)
            outs.append(o)
            lses.append(lse)
        m = functools.reduce(jnp.maximum, lses)
        es = [jnp.exp2(l - m) for l in lses]
        den = functools.reduce(lambda a, b: a + b, es)
        acc = jnp.zeros((tc, D_MODEL), F32)
        for e, o in zip(es, outs):
            hi, mid, _ = split3(e / den)
            w = jnp.dot(jnp.concatenate([hi, mid], axis=1), e_ref[...], preferred_element_type=F32)
            acc = acc + w * o
        rows = slice(c * tc, (c + 1) * tc)
        y_ref[rows, :] = (acc * _silu(z_ref[rows, :].astype(F32))).astype(BF16)


def _dilated_merge(outs, lses, z2, *, n_b, seq, dils):
    tm = 2 * MERGE_CHUNK
    seq_tiles = seq // tm
    expand = np.zeros((LANES, D_MODEL), np.float32)
    for h in range(N_HEADS):
        expand[(h % 2) * HEAD_DIM + (h // 2) * LSE_LANES, h * HEAD_DIM:(h + 1) * HEAD_DIM] = 1.0
    expand = jnp.asarray(np.concatenate([expand, expand], axis=0), BF16)
    omap = lambda i: (i // seq_tiles, 0, i % seq_tiles, 0)
    in_specs = [pl.BlockSpec((None, d, tm // d, D_MODEL), omap) for d in dils]
    in_specs += [pl.BlockSpec((None, d, tm // d, LANES), omap) for d in dils]
    perms = [jnp.asarray(_fold_perm(MERGE_CHUNK, d).T, BF16) for d in dils if d > 1]
    in_specs += [pl.BlockSpec(p.shape, lambda i: (0, 0)) for p in perms]
    in_specs += [pl.BlockSpec(expand.shape, lambda i: (0, 0)), pl.BlockSpec((tm, D_MODEL), lambda i: (i, 0))]
    vmem = (2 * tm * D_MODEL * (3 * 2 + 2 + 2) + 2 * 3 * tm * LANES * 4 + 4 * tm * tm * 2
            + 2 * 2 * LANES * D_MODEL * 2 + 10 * tm * D_MODEL * 4)
    return pl.pallas_call(
        functools.partial(_dilated_merge_kernel, dils=tuple(dils)),
        grid=(n_b * seq // tm,),
        in_specs=in_specs,
        out_specs=pl.BlockSpec((tm, D_MODEL), lambda i: (i, 0)),
        out_shape=jax.ShapeDtypeStruct((n_b * seq, D_MODEL), BF16),
        compiler_params=_params(vmem, 1),
        name="dilated_merge",
    )(*outs, *lses, *perms, expand, z2)


def _moba_kernel(tril_ref, qext_ref, kx_ref, vx_ref, z_ref, y_ref, sa_scr, sb_scr, m_scr, acc_scr):
    t = pl.program_id(2)
    g = 4
    tq = MOBA_BLOCK
    rows_n = g * tq
    lane = lax.broadcasted_iota(jnp.int32, (tq, LANES), 1)
    lo = lane < HEAD_DIM

    def block(ref, blk_idx):
        r0 = pl.multiple_of(blk_idx * MOBA_BLOCK, MOBA_BLOCK)
        return ref[pl.ds(r0, MOBA_BLOCK), :]

    def scores(blk_idx):
        return lax.dot_general(qext_ref[...], block(kx_ref, blk_idx), NT_DIMS, preferred_element_type=F32)

    def blocks_of(j):
        first = jnp.where(j == 0, t, 2 * j - 1)
        second = jnp.where(2 * j + 1 <= t, 2 * j, t + 1)
        return first, second

    def produce(dst, j):
        first, second = blocks_of(j)
        dst[:, 0:tq] = scores(first)
        dst[:, tq:2 * tq] = scores(second)

    def consume(src, j):
        first, second = blocks_of(j)
        sn = src[...]
        m_old = m_scr[...]
        m_new = jnp.maximum(m_old, jnp.max(sn, axis=-1, keepdims=True))
        alpha = jnp.exp2(m_old - m_new)
        pn = jnp.exp2(sn - jnp.concatenate([m_new] * (2 * tq // LANES), axis=1)).astype(BF16)
        acc_scr[...] = (alpha * acc_scr[...]
                        + jnp.dot(pn[:, 0:tq], block(vx_ref, first), preferred_element_type=F32)
                        + jnp.dot(pn[:, tq:2 * tq], block(vx_ref, second), preferred_element_type=F32))
        m_scr[...] = m_new

    n_items = (t + 2) // 2
    tril = tril_ref[...]
    s0 = scores(t)
    sa_scr[:, 0:tq] = jnp.concatenate([s0[gi * tq:(gi + 1) * tq] + tril for gi in range(g)], axis=0)
    sa_scr[:, tq:2 * tq] = scores(blocks_of(0)[1])
    m_scr[...] = jnp.full((rows_n, LANES), NEG_BIG, F32)
    acc_scr[...] = jnp.zeros((rows_n, LANES), F32)

    def stage(src, dst, j):
        if dst is not None:
            produce(dst, jnp.minimum(j + 1, n_items - 1))
        consume(src, j)

    bufs = (sa_scr, sb_scr)

    def group(gg, carry):
        for u in range(MOBA_UNROLL):
            stage(bufs[u % 2], bufs[(u + 1) % 2], MOBA_UNROLL * gg + u)
        return carry

    lax.fori_loop(0, n_items // MOBA_UNROLL, group, 0)
    rem = n_items % MOBA_UNROLL
    base = n_items - rem
    for u in range(MOBA_UNROLL - 1):
        if u < MOBA_UNROLL - 2:
            pl.when(rem > u + 1)(functools.partial(stage, bufs[u % 2], bufs[(u + 1) % 2], base + u))
        pl.when(rem == u + 1)(functools.partial(stage, bufs[u % 2], None, base + u))

    acc = acc_scr[...]
    for pi in range(g // 2):
        oe = acc[(2 * pi) * tq:(2 * pi + 1) * tq]
        oo = acc[(2 * pi + 1) * tq:(2 * pi + 2) * tq]
        o_slab = jnp.where(lo, oe / pltpu.roll(oe, HEAD_DIM, 1), pltpu.roll(oo, HEAD_DIM, 1) / oo)
        zf = z_ref[:, pi * LANES:(pi + 1) * LANES].astype(F32)
        y_ref[:, pi * LANES:(pi + 1) * LANES] = (o_slab * _silu(zf)).astype(BF16)


def _moba_attention(qext, kx, vx, z):
    n_b, seq, _ = z.shape
    n_kvh = kx.shape[2] // LANES
    g = N_HEADS // n_kvh
    tq = MOBA_BLOCK
    qw = g * HEAD_DIM
    tril = jnp.asarray(np.where(np.tril(np.ones((tq, tq), bool)), 0.0, NEG_BIG), F32)
    tile = lambda b, j, t: (b, t, j)
    whole = lambda b, j, t: (b, 0, j)
    vmem = (2 * 2 * seq * LANES * 2 + 2 * 2 * tq * qw * 2 + tq * tq * 4 * 2
            + g * tq * LANES * (2 * 2 + 4 + 4) + 2 * g * tq * 2 * tq * 4 + 6 * g * tq * 2 * tq * 4)
    return pl.pallas_call(
        _moba_kernel,
        grid=(n_b, n_kvh, seq // tq),
        in_specs=[pl.BlockSpec((tq, tq), lambda b, j, t: (0, 0)),
                  pl.BlockSpec((None, None, g * tq, LANES), lambda b, j, t: (b, j, t, 0)),
                  pl.BlockSpec((None, seq, LANES), whole), pl.BlockSpec((None, seq, LANES), whole),
                  pl.BlockSpec((None, tq, qw), tile)],
        out_specs=pl.BlockSpec((None, tq, qw), tile),
        out_shape=jax.ShapeDtypeStruct((n_b, seq, D_MODEL), BF16),
        scratch_shapes=[pltpu.VMEM((g * tq, 2 * tq), F32), pltpu.VMEM((g * tq, 2 * tq), F32),
                        pltpu.VMEM((g * tq, LANES), F32), pltpu.VMEM((g * tq, LANES), F32)],
        compiler_params=_params(vmem, 3),
        name="moba_attention",
    )(tril, qext, kx, vx, z)


def _rope_base(seq):
    half = ROT_DIM // 2
    inv = ROPE_THETA ** (-jnp.arange(0, ROT_DIM, 2, dtype=F32) / ROT_DIM)
    c = np.arange(LANES) % HEAD_DIM
    rot = c < ROT_DIM
    freq = jnp.where(rot, inv[c % half], 0.0)
    ang = jnp.arange(seq, dtype=F32)[:, None] * freq[None, :]
    cos, sin = jnp.cos(ang), jnp.sin(ang)
    first = jnp.asarray(c < half, F32)
    second = jnp.asarray(rot & (c >= half), F32)
    return jnp.stack([cos, -sin * first, sin * second])


def _rope_table(tab, dil):
    if dil == 1:
        return tab
    seq = tab.shape[1]
    return tab.reshape(3, seq // TM, TM // dil, dil, LANES).transpose(0, 1, 3, 2, 4).reshape(3, seq, LANES)


def _prep_w_in(w_in, kind):
    kv = KV_HEADS[kind] * HEAD_DIM
    n_grp = len(DILATED_GROUPS) if kind == 1 else 1
    scale = np.ones((w_in.shape[1],), np.float32)
    for g in range(n_grp):
        off = g * (D_MODEL + 2 * kv)
        scale[off:off + D_MODEL] = Q_SCALE
    return (w_in * jnp.asarray(scale)).astype(BF16)


def kernel(x, w_in_0, sink_0, w_out_0, ln_g_0, ln_b_0, w_in_1, w_out_1, ln_g_1, ln_b_1, w_in_2, w_out_2, ln_g_2, ln_b_2, w_in_3, sink_3, w_out_3, ln_g_3, ln_b_3):
    n_b, seq, d_model = x.shape
    assert d_model == D_MODEL and seq % max(dl * BAND for _, dl in DILATED_GROUPS) == 0 and seq % TM == 0
    assert (seq // MOBA_BLOCK) % 8 == 0 and seq // MOBA_BLOCK <= LANES - HEAD_DIM
    layers = [(w_in_0, sink_0, w_out_0, ln_g_0, ln_b_0),
              (w_in_1, None, w_out_1, ln_g_1, ln_b_1),
              (w_in_2, None, w_out_2, ln_g_2, ln_b_2),
              (w_in_3, sink_3, w_out_3, ln_g_3, ln_b_3)]
    dil_b = [dl for _, dl in DILATED_GROUPS]
    rope_nat = _rope_base(seq)
    x2 = x.reshape(n_b * seq, D_MODEL)
    for i, (w_in, sink, w_out, ln_g, ln_b) in enumerate(layers):
        kind = i % N_MIXERS
        w = _prep_w_in(w_in, kind)
        nat = lambda a: a.reshape(n_b, seq, a.shape[-1])
        if kind == 1:
            tabs = [_rope_table(rope_nat, dl) for dl in dil_b]
            grp, z = _inproj(x2, tabs, w, n_b=n_b, seq=seq, dils=dil_b, n_kvh=KV_HEADS[kind], moba=False)
            outs, lses = [], []
            for gi, ((window, dl), (q, kx, vx)) in enumerate(zip(DILATED_GROUPS, grp)):
                o, lse = _band_attention(q, kx, vx, max_dist=window // dl, name="dilated_group_%d" % gi)
                outs.append(o)
                lses.append(lse)
            y = _dilated_merge(outs, lses, z, n_b=n_b, seq=seq, dils=dil_b)
        else:
            grp, z = _inproj(x2, [rope_nat], w, n_b=n_b, seq=seq, dils=[1], n_kvh=KV_HEADS[kind],
                             moba=(kind == 2))
            q, kx, vx = grp[0]
            if kind == 0:
                y = _band_attention(q, kx, vx, max_dist=WINDOW_A - 1, sink=sink,
                                    z=z.reshape(n_b, 1, seq, D_MODEL), name="swa_attention")
            else:
                y = _moba_attention(q, nat(kx), nat(vx), nat(z))
        x2 = _outproj(y.reshape(n_b * seq, D_MODEL), x2, w_out.astype(BF16),
                      ln_g.reshape(1, D_MODEL).astype(F32), ln_b.reshape(1, D_MODEL).astype(F32))
    return x2.reshape(n_b, seq, D_MODEL)
```

```python
import functools
import math

import numpy as np
import jax
import jax.numpy as jnp
from jax import lax
from jax.experimental import pallas as pl
from jax.experimental.pallas import tpu as pltpu

D_MODEL = 1024
HEAD_DIM = 64
N_HEADS = D_MODEL // HEAD_DIM
ROT_DIM = HEAD_DIM // 4
ROPE_THETA = 500000.0
DEPTH = 4
N_MIXERS = 3
KV_HEADS = (2, 4, 4)
WINDOW_A = 128
DILATED_GROUPS = ((128, 1), (512, 4), (2048, 16))
BAND = 128
MOBA_BLOCK = 256
MOBA_TOPK = 3
DEEPNORM_ALPHA = (2 * DEPTH) ** 0.25
LN_EPS = 1e-5
LOG2E = math.log2(math.e)
Q_SCALE = HEAD_DIM ** -0.5 * LOG2E

LANES = 128
LSE_LANES = LANES // N_HEADS
NEG_BIG = -1e30
VMEM_CAP = 60000 * 1024
TM = 512
MERGE_CHUNK = 256
BAND_TILE = 2048
MOBA_UNROLL = 4

F32 = jnp.float32
BF16 = jnp.bfloat16
NT_DIMS = (((1,), (1,)), ((), ()))


def _params(vmem_bytes, n_grid):
    limit = int(min(max(2 * vmem_bytes, 32 * 1024 * 1024), VMEM_CAP))
    return pltpu.CompilerParams(dimension_semantics=("arbitrary",) * n_grid,
                                vmem_limit_bytes=limit)


def _silu(z):
    return z / (1.0 + jnp.exp(-z))


def _fold_perm(n, dil):
    p = np.zeros((n, n), np.float32)
    idx = np.arange(n)
    p[(idx % dil) * (n // dil) + idx // dil, idx] = 1.0
    return p


def _moba_bias(km, qrows, t, n_blk):
    rows_n = qrows.shape[0]
    gate = lax.dot_general(km, qrows, NT_DIMS, preferred_element_type=F32)
    blk = lax.broadcasted_iota(jnp.int32, gate.shape, 0).astype(F32)
    tf = t.astype(F32)
    gate = jnp.where(blk < tf, gate, -jnp.inf)
    sel = jnp.zeros(gate.shape, F32)
    for _ in range(MOBA_TOPK):
        mx = jnp.max(gate, axis=0, keepdims=True)
        is_max = (gate == mx) & (mx > -jnp.inf)
        first = jnp.min(jnp.where(is_max, blk, float(n_blk)), axis=0, keepdims=True)
        pick = blk == first
        sel = jnp.where(pick, 1.0, sel)
        gate = jnp.where(pick, -jnp.inf, gate)
    bias_t = jnp.where((sel > 0.0) | (blk == tf), 0.0, NEG_BIG)
    bias_full = jnp.concatenate(
        [jnp.zeros((HEAD_DIM, rows_n), F32), bias_t,
         jnp.zeros((LANES - HEAD_DIM - n_blk, rows_n), F32)], axis=0)
    return bias_full.T.astype(BF16)


def _inproj_kernel(*refs, dils, n_kvh, moba, seq_tiles):
    n_grp = len(dils)
    kv = n_kvh * HEAD_DIM
    it = iter(refs)
    x_ref = next(it)
    groups = [(d, next(it) if d > 1 else None, next(it)) for d in dils]
    w_ref = next(it)
    outs = [(next(it), next(it), next(it)) for _ in range(n_grp)]
    z_ref = next(it)

    def proj(lhs, col, width):
        return jnp.dot(lhs, w_ref[:, col:col + width], preferred_element_type=F32)

    tm = x_ref.shape[0]
    xb = x_ref[...].astype(BF16)
    lane = lax.broadcasted_iota(jnp.int32, (tm, LANES), 1)
    lo = lane < HEAD_DIM
    if moba:
        km_scr = next(it)
        tile = pl.program_id(0) % seq_tiles
        n_q = tm // MOBA_BLOCK
        n_blk = seq_tiles * n_q
        row = lax.broadcasted_iota(jnp.int32, (tm, LANES), 0)
        pos = tile * tm + row
        k_aux = jnp.where(lane == HEAD_DIM + lax.shift_right_logical(pos, 8), 1.0, 0.0)

        @pl.when(tile == 0)
        def _new_sequence():
            km_scr[...] = jnp.zeros(km_scr.shape, F32)

    off = 0
    for (d, perm_ref, rope_ref), (q_ref, kx_ref, vx_ref) in zip(groups, outs):
        n = tm // d
        xg = xb if d == 1 else jnp.dot(perm_ref[...], xb, preferred_element_type=F32).astype(BF16)
        cs, sa, sb = rope_ref[0], rope_ref[1], rope_ref[2]

        def rope(t):
            return (t * cs + pltpu.roll(t, LANES - ROT_DIM // 2, 1) * sa
                    + pltpu.roll(t, ROT_DIM // 2, 1) * sb)

        def put(ref, col, val):
            for r in range(d):
                ref[r, :, col:col + val.shape[1]] = val[r * n:(r + 1) * n]

        acc_k = proj(xg, off + D_MODEL, kv)
        acc_v = proj(xg, off + D_MODEL + kv, kv)
        for c in range(n_kvh // 2):
            k2 = rope(acc_k[:, c * LANES:(c + 1) * LANES])
            v2 = acc_v[:, c * LANES:(c + 1) * LANES]
            k2r = pltpu.roll(k2, HEAD_DIM, 1)
            v2r = pltpu.roll(v2, HEAD_DIM, 1)
            for e in range(2):
                j = 2 * c + e
                if moba:
                    ke = jnp.where(lo, k2 if e == 0 else k2r, 0.0)
                    for bq in range(n_q):
                        mean = jnp.sum(ke[bq * MOBA_BLOCK:(bq + 1) * MOBA_BLOCK], axis=0,
                                       keepdims=True) * (1.0 / MOBA_BLOCK)
                        km_scr[j, pl.ds(HEAD_DIM + tile * n_q + bq, 1), :] = mean
                    kk = jnp.where(lo, ke, k_aux)
                    vv = jnp.where(lo, v2 if e == 0 else v2r, 1.0)
                else:
                    kk = jnp.where(lo, k2, k2r) if e == 0 else jnp.where(lo, k2r, k2)
                    vv = jnp.where(lo, v2, v2r) if e == 0 else jnp.where(lo, v2r, v2)
                put(kx_ref, j * LANES, kk.astype(BF16))
                put(vx_ref, j * LANES, vv.astype(BF16))

        for c in range(D_MODEL // 512):
            acc = proj(xg, off + c * 512, 512)
            heads = []
            for s in range(4):
                q2 = rope(acc[:, s * LANES:(s + 1) * LANES])
                if moba:
                    heads.append(jnp.where(lo, q2, 0.0).astype(BF16))
                    heads.append(jnp.where(lo, pltpu.roll(q2, HEAD_DIM, 1), 0.0).astype(BF16))
                else:
                    put(q_ref, c * 512 + s * LANES, q2.astype(BF16))
            if moba:
                g = N_HEADS // n_kvh
                for e in range(len(heads) // g):
                    j = c * (len(heads) // g) + e
                    km = km_scr[j, HEAD_DIM:HEAD_DIM + n_blk, :].astype(BF16)
                    for bq in range(n_q):
                        rows = slice(bq * MOBA_BLOCK, (bq + 1) * MOBA_BLOCK)
                        qrows = jnp.concatenate([heads[e * g + gi][rows] for gi in range(g)], axis=0)
                        bias = _moba_bias(km, qrows, tile * n_q + bq, n_blk)
                        q_ref[j, bq * g * MOBA_BLOCK:(bq + 1) * g * MOBA_BLOCK, :] = qrows + bias
        off += D_MODEL + 2 * kv

    for c in range(D_MODEL // 512):
        z_ref[:, c * 512:(c + 1) * 512] = proj(xb, off + c * 512, 512).astype(BF16)


def _inproj(x2, rope_tabs, w, *, n_b, seq, dils, n_kvh, moba):
    tm = TM
    seq_tiles = seq // tm
    kw = n_kvh * LANES
    const = lambda a: pl.BlockSpec(a.shape, lambda i: (0, 0), pipeline_mode=pl.Buffered(1))
    in_specs = [pl.BlockSpec((tm, D_MODEL), lambda i: (i, 0))]
    args = [x2]
    for d, tab in zip(dils, rope_tabs):
        if d > 1:
            perm = jnp.asarray(_fold_perm(tm, d), BF16)
            in_specs.append(const(perm))
            args.append(perm)
        in_specs.append(pl.BlockSpec((3, tm, LANES), lambda i: (0, i % seq_tiles, 0)))
        args.append(tab)
    in_specs.append(const(w))
    args.append(w)
    out_specs, out_shape = [], []
    g = N_HEADS // n_kvh
    for d in dils:
        omap = lambda i: (i // seq_tiles, 0, i % seq_tiles, 0)
        for width in (D_MODEL, kw, kw):
            out_specs.append(pl.BlockSpec((None, d, tm // d, width), omap))
            out_shape.append(jax.ShapeDtypeStruct((n_b, d, seq // d, width), BF16))
    if moba:
        out_specs[0] = pl.BlockSpec((None, n_kvh, tm * g, LANES), omap)
        out_shape[0] = jax.ShapeDtypeStruct((n_b, n_kvh, seq * g, LANES), BF16)
    out_specs.append(pl.BlockSpec((tm, D_MODEL), lambda i: (i, 0)))
    out_shape.append(jax.ShapeDtypeStruct((n_b * seq, D_MODEL), BF16))
    scratch = [pltpu.VMEM((n_kvh, LANES, LANES), F32)] if moba else []
    vmem = (2 * tm * D_MODEL * 4 + len(dils) * (2 * 3 * tm * LANES * 4 + tm * tm * 2)
            + 2 * D_MODEL * w.shape[1] + 2 * 2 * tm * (len(dils) * (2 * D_MODEL + 2 * kw) + D_MODEL)
            + 6 * tm * 512 * 4)
    res = pl.pallas_call(
        functools.partial(_inproj_kernel, dils=tuple(dils), n_kvh=n_kvh, moba=moba, seq_tiles=seq_tiles),
        grid=(n_b * seq // tm,),
        in_specs=in_specs, out_specs=out_specs, out_shape=out_shape, scratch_shapes=scratch,
        compiler_params=_params(vmem, 1),
        name="inproj",
    )(*args)
    return [tuple(res[3 * g:3 * g + 3]) for g in range(len(dils))], res[-1]


def _outproj_kernel(y_ref, x_ref, w_ref, g_ref, b_ref, o_ref):
    rows = 256
    for c in range(y_ref.shape[0] // rows):
        sl = slice(c * rows, (c + 1) * rows)
        t = DEEPNORM_ALPHA * x_ref[sl, :] + jnp.dot(y_ref[sl, :], w_ref[...], preferred_element_type=F32)
        mu = jnp.mean(t, axis=-1, keepdims=True)
        d = t - mu
        var = jnp.mean(d * d, axis=-1, keepdims=True)
        o_ref[sl, :] = d * lax.rsqrt(var + LN_EPS) * g_ref[...] + b_ref[...]


def _outproj(y2, x2, w, g, b):
    t_rows = x2.shape[0]
    tm = 4 * TM
    row = lambda i: (i, 0)
    full = lambda i: (0, 0)
    vmem = 2 * tm * D_MODEL * (2 + 4 + 4) + 2 * D_MODEL * D_MODEL * 2 + 3 * 128 * D_MODEL * 4
    return pl.pallas_call(
        _outproj_kernel,
        grid=(t_rows // tm,),
        in_specs=[pl.BlockSpec((tm, D_MODEL), row), pl.BlockSpec((tm, D_MODEL), row),
                  pl.BlockSpec((D_MODEL, D_MODEL), full),
                  pl.BlockSpec((1, D_MODEL), full), pl.BlockSpec((1, D_MODEL), full)],
        out_specs=pl.BlockSpec((tm, D_MODEL), row),
        out_shape=jax.ShapeDtypeStruct((t_rows, D_MODEL), F32),
        compiler_params=_params(vmem, 1),
        name="outproj_ln",
    )(y2, x2, w, g, b)


def _band_consts(max_dist, sink):
    i = np.arange(BAND)[:, None]
    c = np.arange(2 * BAND)[None, :]
    dist = BAND + i - c
    ok = (dist >= 0) & (dist <= max_dist)
    first = ok & (c >= BAND)
    sel = np.zeros((2, 2 * BAND, LANES), np.float32)
    sel[0, :, :HEAD_DIM] = 1.0
    sel[1, :, HEAD_DIM:] = 1.0
    ones = sel.copy()
    if sink is not None:
        assert not ok[:, 0].any()
        ok[:, 0] = True
        first[:, 0] = True
        sel[:, 0, :] = 0.0
    mask = np.where(np.stack([first, ok]), 0.0, NEG_BIG)
    consts = [jnp.asarray(np.concatenate([mask, mask], axis=2), F32),
              jnp.asarray(sel, BF16), jnp.asarray(ones, BF16)]
    if sink is not None:
        s2 = sink.astype(F32) * LOG2E
        hi = s2.astype(BF16)
        lo = (s2 - hi.astype(F32)).astype(BF16)
        per_pair = jnp.stack([hi[0::2], lo[0::2], hi[1::2], lo[1::2]], axis=1)
        per_pair = jnp.pad(per_pair, ((0, 0), (0, LANES - 4)))
        k_aux = np.zeros((2, 2 * BAND, LANES), np.float32)
        k_aux[0, 0, 0:2] = 1.0
        k_aux[1, 0, 2:4] = 1.0
        consts += [jnp.repeat(per_pair, BAND, axis=0), jnp.asarray(k_aux, BF16)]
    return consts


def _band_kernel(*refs, n_kvh, has_sink):
    if has_sink:
        (mask_ref, sel_ref, ones_ref, qaux_ref, kaux_ref, q_ref, kc_ref, kp_ref, vc_ref, vp_ref, z_ref,
         y_ref, kbuf, vbuf) = refs
    else:
        mask_ref, sel_ref, ones_ref, q_ref, kc_ref, kp_ref, vc_ref, vp_ref, o_ref, lse_ref, kbuf, vbuf = refs
    n_pair = N_HEADS // n_kvh // 2
    n_seq, tq = q_ref.shape[0], q_ref.shape[1]
    i_tile = pl.program_id(2)
    kbuf[:, 0:BAND, :] = kp_ref[...]
    kbuf[:, BAND:BAND + tq, :] = kc_ref[...]
    vbuf[:, 0:BAND, :] = vp_ref[...]
    vbuf[:, BAND:BAND + tq, :] = vc_ref[...]
    lane = lax.broadcasted_iota(jnp.int32, (BAND, LANES), 1)
    lo = lane < HEAD_DIM
    mask_first = mask_ref[jnp.where(i_tile == 0, 0, 1)]
    mask_rest = mask_ref[1]

    for rr in range(n_seq):
        for bb in range(tq // BAND):
            r0 = bb * BAND
            mask = mask_first if bb == 0 else mask_rest
            lse_acc = jnp.zeros((BAND, LANES), F32)
            for j in range(n_kvh):
                kcat = kbuf[rr, r0:r0 + 2 * BAND, j * LANES:(j + 1) * LANES]
                vcat = vbuf[rr, r0:r0 + 2 * BAND, j * LANES:(j + 1) * LANES]
                k_e, k_o = kcat * sel_ref[0], kcat * sel_ref[1]
                v_e = jnp.concatenate([vcat * sel_ref[0], ones_ref[0]], axis=1)
                v_o = jnp.concatenate([vcat * sel_ref[1], ones_ref[1]], axis=1)
                qp = jnp.concatenate(
                    [q_ref[rr, r0:r0 + BAND, (j * n_pair + pi) * LANES:(j * n_pair + pi + 1) * LANES]
                     for pi in range(n_pair)], axis=0)
                if has_sink:
                    qp = jnp.concatenate([qp, qaux_ref[j * n_pair * BAND:(j + 1) * n_pair * BAND, :]], axis=1)
                    k_e = jnp.concatenate([k_e, kaux_ref[0]], axis=1)
                    k_o = jnp.concatenate([k_o, kaux_ref[1]], axis=1)
                kbd = jnp.concatenate([k_e, k_o], axis=0)
                vbd = jnp.concatenate([v_e, v_o], axis=0)
                s = lax.dot_general(qp, kbd, NT_DIMS, preferred_element_type=F32)
                ps, ms = [], []
                for pi in range(n_pair):
                    sp = s[pi * BAND:(pi + 1) * BAND] + mask
                    m_e = jnp.max(sp[:, 0:2 * BAND], axis=-1, keepdims=True)
                    m_o = jnp.max(sp[:, 2 * BAND:], axis=-1, keepdims=True)
                    m_full = jnp.concatenate([jnp.broadcast_to(m_e, (BAND, 2 * BAND)),
                                              jnp.broadcast_to(m_o, (BAND, 2 * BAND))], axis=1)
                    ms.append((m_e, m_o))
                    ps.append(jnp.exp2(sp - m_full).astype(BF16))
                p = jnp.concatenate(ps, axis=0)
                oext = jnp.dot(p, vbd, preferred_element_type=F32)
                for pi in range(n_pair):
                    oe = oext[pi * BAND:(pi + 1) * BAND]
                    o_slab = oe[:, 0:LANES] / oe[:, LANES:]
                    pair = j * n_pair + pi
                    col = pair * LANES
                    if has_sink:
                        zf = z_ref[rr, r0:r0 + BAND, col:col + LANES].astype(F32)
                        y_ref[rr, r0:r0 + BAND, col:col + LANES] = (o_slab * _silu(zf)).astype(BF16)
                    else:
                        o_ref[rr, r0:r0 + BAND, col:col + LANES] = o_slab.astype(BF16)
                        lse_slab = jnp.where(lo, ms[pi][0], ms[pi][1]) + jnp.log2(oe[:, LANES:])
                        a, b = pair * LSE_LANES, HEAD_DIM + pair * LSE_LANES
                        mine = ((lane >= a) & (lane < a + LSE_LANES)) | ((lane >= b) & (lane < b + LSE_LANES))
                        lse_acc = jnp.where(mine, lse_slab, lse_acc)
            if not has_sink:
                lse_ref[rr, r0:r0 + BAND, :] = lse_acc


def _band_attention(q, kx, vx, *, max_dist, sink=None, z=None, name):
    n_b, n_seq, length, _ = q.shape
    kw = kx.shape[3]
    n_kvh = kw // LANES
    tq = min(BAND_TILE, length)
    per_step = min(n_seq, BAND_TILE // tq)
    bpt = tq // BAND
    cur = lambda b, r, i: (b, r, i, 0)
    prev = lambda b, r, i: (b, r, jnp.maximum(i * bpt - 1, 0), 0)
    consts = _band_consts(max_dist, sink)
    const_specs = [pl.BlockSpec(c.shape, lambda b, r, i, n=c.ndim: (0,) * n) for c in consts]
    wide = pl.BlockSpec((None, per_step, tq, D_MODEL), cur)
    in_specs = const_specs + [
        wide,
        pl.BlockSpec((None, per_step, tq, kw), cur), pl.BlockSpec((None, per_step, BAND, kw), prev),
        pl.BlockSpec((None, per_step, tq, kw), cur), pl.BlockSpec((None, per_step, BAND, kw), prev)]
    args = [*consts, q, kx, kx, vx, vx]
    shape = (n_b, n_seq, length, D_MODEL)
    if sink is not None:
        in_specs.append(wide)
        args.append(z)
        out_specs, out_shape = wide, jax.ShapeDtypeStruct(shape, BF16)
    else:
        out_specs = [wide, pl.BlockSpec((None, per_step, tq, LANES), cur)]
        out_shape = [jax.ShapeDtypeStruct(shape, BF16),
                     jax.ShapeDtypeStruct((n_b, n_seq, length, LANES), F32)]
    rows = per_step * tq
    vmem = (2 * rows * D_MODEL * 2 * 3 + 2 * rows * LANES * 4 + 6 * per_step * (tq + BAND) * kw * 2 * 2
            + 2 * sum(int(np.prod(c.shape)) * c.dtype.itemsize for c in consts) + 16 * 4 * BAND * 4 * BAND * 4)
    return pl.pallas_call(
        functools.partial(_band_kernel, n_kvh=n_kvh, has_sink=sink is not None),
        grid=(n_b, n_seq // per_step, length // tq),
        in_specs=in_specs, out_specs=out_specs, out_shape=out_shape,
        scratch_shapes=[pltpu.VMEM((per_step, tq + BAND, kw), BF16), pltpu.VMEM((per_step, tq + BAND, kw), BF16)],
        compiler_params=_params(vmem, 3),
        name=name,
    )(*args)


def _dilated_merge_kernel(*refs, dils):
    n_grp = len(dils)
    it = iter(refs)
    o_refs = [next(it) for _ in range(n_grp)]
    l_refs = [next(it) for _ in range(n_grp)]
    p_refs = [next(it) if d > 1 else None for d in dils]
    e_ref, z_ref, y_ref = next(it), next(it), next(it)
    tc = MERGE_CHUNK

    def split3(a):
        hi = a.astype(BF16)
        r1 = a - hi.astype(F32)
        mid = r1.astype(BF16)
        return hi, mid, (r1 - mid.astype(F32)).astype(BF16)

    for c in range(z_ref.shape[0] // tc):
        outs, lses = [], []
        for d, o_ref, l_ref, p_ref in zip(dils, o_refs, l_refs, p_refs):
            fold = slice(c * tc // d, (c + 1) * tc // d)
            o = o_ref[:, fold, :].reshape(tc, D_MODEL)
            lse = l_ref[:, fold, :].reshape(tc, LANES)
            if d > 1:
                pt = p_ref[...]
                o = jnp.dot(pt, o, preferred_element_type=F32)
                parts = jnp.concatenate(split3(lse), axis=1)
                lp = jnp.dot(pt, parts, preferred_element_type=F32)
                lse = lp[:, 0:LANES] + lp[:, LANES:2 * LANES] + lp[:, 2 * LANES:]
            else:
                o = o.astype(F32)
            outs.append(o)
            lses.append(lse)
        m = functools.reduce(jnp.maximum, lses)
        es = [jnp.exp2(l - m) for l in lses]
        den = functools.reduce(lambda a, b: a + b, es)
        acc = jnp.zeros((tc, D_MODEL), F32)
        for e, o in zip(es, outs):
            hi, mid, _ = split3(e / den)
            w = jnp.dot(jnp.concatenate([hi, mid], axis=1), e_ref[...], preferred_element_type=F32)
            acc = acc + w * o
        rows = slice(c * tc, (c + 1) * tc)
        y_ref[rows, :] = (acc * _silu(z_ref[rows, :].astype(F32))).astype(BF16)


def _dilated_merge(outs, lses, z2, *, n_b, seq, dils):
    tm = 2 * MERGE_CHUNK
    seq_tiles = seq // tm
    expand = np.zeros((LANES, D_MODEL), np.float32)
    for h in range(N_HEADS):
        expand[(h % 2) * HEAD_DIM + (h // 2) * LSE_LANES, h * HEAD_DIM:(h + 1) * HEAD_DIM] = 1.0
    expand = jnp.asarray(np.concatenate([expand, expand], axis=0), BF16)
    omap = lambda i: (i // seq_tiles, 0, i % seq_tiles, 0)
    in_specs = [pl.BlockSpec((None, d, tm // d, D_MODEL), omap) for d in dils]
    in_specs += [pl.BlockSpec((None, d, tm // d, LANES), omap) for d in dils]
    perms = [jnp.asarray(_fold_perm(MERGE_CHUNK, d).T, BF16) for d in dils if d > 1]
    in_specs += [pl.BlockSpec(p.shape, lambda i: (0, 0)) for p in perms]
    in_specs += [pl.BlockSpec(expand.shape, lambda i: (0, 0)), pl.BlockSpec((tm, D_MODEL), lambda i: (i, 0))]
    vmem = (2 * tm * D_MODEL * (3 * 2 + 2 + 2) + 2 * 3 * tm * LANES * 4 + 4 * tm * tm * 2
            + 2 * 2 * LANES * D_MODEL * 2 + 10 * tm * D_MODEL * 4)
    return pl.pallas_call(
        functools.partial(_dilated_merge_kernel, dils=tuple(dils)),
        grid=(n_b * seq // tm,),
        in_specs=in_specs,
        out_specs=pl.BlockSpec((tm, D_MODEL), lambda i: (i, 0)),
        out_shape=jax.ShapeDtypeStruct((n_b * seq, D_MODEL), BF16),
        compiler_params=_params(vmem, 1),
        name="dilated_merge",
    )(*outs, *lses, *perms, expand, z2)


def _moba_kernel(tril_ref, qext_ref, kx_ref, vx_ref, z_ref, y_ref, sa_scr, sb_scr, m_scr, acc_scr):
    t = pl.program_id(2)
    g = 4
    tq = MOBA_BLOCK
    rows_n = g * tq
    lane = lax.broadcasted_iota(jnp.int32, (tq, LANES), 1)
    lo = lane < HEAD_DIM

    def block(ref, blk_idx):
        r0 = pl.multiple_of(blk_idx * MOBA_BLOCK, MOBA_BLOCK)
        return ref[pl.ds(r0, MOBA_BLOCK), :]

    def scores(blk_idx):
        return lax.dot_general(qext_ref[...], block(kx_ref, blk_idx), NT_DIMS, preferred_element_type=F32)

    def blocks_of(j):
        first = jnp.where(j == 0, t, 2 * j - 1)
        second = jnp.where(2 * j + 1 <= t, 2 * j, t + 1)
        return first, second

    def produce(dst, j):
        first, second = blocks_of(j)
        dst[:, 0:tq] = scores(first)
        dst[:, tq:2 * tq] = scores(second)

    def consume(src, j):
        first, second = blocks_of(j)
        sn = src[...]
        m_old = m_scr[...]
        m_new = jnp.maximum(m_old, jnp.max(sn, axis=-1, keepdims=True))
        alpha = jnp.exp2(m_old - m_new)
        pn = jnp.exp2(sn - jnp.concatenate([m_new] * (2 * tq // LANES), axis=1)).astype(BF16)
        acc_scr[...] = (alpha * acc_scr[...]
                        + jnp.dot(pn[:, 0:tq], block(vx_ref, first), preferred_element_type=F32)
                        + jnp.dot(pn[:, tq:2 * tq], block(vx_ref, second), preferred_element_type=F32))
        m_scr[...] = m_new

    n_items = (t + 2) // 2
    tril = tril_ref[...]
    s0 = scores(t)
    sa_scr[:, 0:tq] = jnp.concatenate([s0[gi * tq:(gi + 1) * tq] + tril for gi in range(g)], axis=0)
    sa_scr[:, tq:2 * tq] = scores(blocks_of(0)[1])
    m_scr[...] = jnp.full((rows_n, LANES), NEG_BIG, F32)
    acc_scr[...] = jnp.zeros((rows_n, LANES), F32)

    def stage(src, dst, j):
        if dst is not None:
            produce(dst, jnp.minimum(j + 1, n_items - 1))
        consume(src, j)

    bufs = (sa_scr, sb_scr)

    def group(gg, carry):
        for u in range(MOBA_UNROLL):
            stage(bufs[u % 2], bufs[(u + 1) % 2], MOBA_UNROLL * gg + u)
        return carry

    lax.fori_loop(0, n_items // MOBA_UNROLL, group, 0)
    rem = n_items % MOBA_UNROLL
    base = n_items - rem
    for u in range(MOBA_UNROLL - 1):
        if u < MOBA_UNROLL - 2:
            pl.when(rem > u + 1)(functools.partial(stage, bufs[u % 2], bufs[(u + 1) % 2], base + u))
        pl.when(rem == u + 1)(functools.partial(stage, bufs[u % 2], None, base + u))

    acc = acc_scr[...]
    for pi in range(g // 2):
        oe = acc[(2 * pi) * tq:(2 * pi + 1) * tq]
        oo = acc[(2 * pi + 1) * tq:(2 * pi + 2) * tq]
        o_slab = jnp.where(lo, oe / pltpu.roll(oe, HEAD_DIM, 1), pltpu.roll(oo, HEAD_DIM, 1) / oo)
        zf = z_ref[:, pi * LANES:(pi + 1) * LANES].astype(F32)
        y_ref[:, pi * LANES:(pi + 1) * LANES] = (o_slab * _silu(zf)).astype(BF16)


def _moba_attention(qext, kx, vx, z):
    n_b, seq, _ = z.shape
    n_kvh = kx.shape[2] // LANES
    g = N_HEADS // n_kvh
    tq = MOBA_BLOCK
    qw = g * HEAD_DIM
    tril = jnp.asarray(np.where(np.tril(np.ones((tq, tq), bool)), 0.0, NEG_BIG), F32)
    tile = lambda b, j, t: (b, t, j)
    whole = lambda b, j, t: (b, 0, j)
    vmem = (2 * 2 * seq * LANES * 2 + 2 * 2 * tq * qw * 2 + tq * tq * 4 * 2
            + g * tq * LANES * (2 * 2 + 4 + 4) + 2 * g * tq * 2 * tq * 4 + 6 * g * tq * 2 * tq * 4)
    return pl.pallas_call(
        _moba_kernel,
        grid=(n_b, n_kvh, seq // tq),
        in_specs=[pl.BlockSpec((tq, tq), lambda b, j, t: (0, 0)),
                  pl.BlockSpec((None, None, g * tq, LANES), lambda b, j, t: (b, j, t, 0)),
                  pl.BlockSpec((None, seq, LANES), whole), pl.BlockSpec((None, seq, LANES), whole),
                  pl.BlockSpec((None, tq, qw), tile)],
        out_specs=pl.BlockSpec((None, tq, qw), tile),
        out_shape=jax.ShapeDtypeStruct((n_b, seq, D_MODEL), BF16),
        scratch_shapes=[pltpu.VMEM((g * tq, 2 * tq), F32), pltpu.VMEM((g * tq, 2 * tq), F32),
                        pltpu.VMEM((g * tq, LANES), F32), pltpu.VMEM((g * tq, LANES), F32)],
        compiler_params=_params(vmem, 3),
        name="moba_attention",
    )(tril, qext, kx, vx, z)


def _rope_base(seq):
    half = ROT_DIM // 2
    inv = ROPE_THETA ** (-jnp.arange(0, ROT_DIM, 2, dtype=F32) / ROT_DIM)
    c = np.arange(LANES) % HEAD_DIM
    rot = c < ROT_DIM
    freq = jnp.where(rot, inv[c % half], 0.0)
    ang = jnp.arange(seq, dtype=F32)[:, None] * freq[None, :]
    cos, sin = jnp.cos(ang), jnp.sin(ang)
    first = jnp.asarray(c < half, F32)
    second = jnp.asarray(rot & (c >= half), F32)
    return jnp.stack([cos, -sin * first, sin * second])


def _rope_table(tab, dil):
    if dil == 1:
        return tab
    seq = tab.shape[1]
    return tab.reshape(3, seq // TM, TM // dil, dil, LANES).transpose(0, 1, 3, 2, 4).reshape(3, seq, LANES)


def _prep_w_in(w_in, kind):
    kv = KV_HEADS[kind] * HEAD_DIM
    n_grp = len(DILATED_GROUPS) if kind == 1 else 1
    scale = np.ones((w_in.shape[1],), np.float32)
    for g in range(n_grp):
        off = g * (D_MODEL + 2 * kv)
        scale[off:off + D_MODEL] = Q_SCALE
    return (w_in * jnp.asarray(scale)).astype(BF16)


def kernel(x, w_in_0, sink_0, w_out_0, ln_g_0, ln_b_0, w_in_1, w_out_1, ln_g_1, ln_b_1, w_in_2, w_out_2, ln_g_2, ln_b_2, w_in_3, sink_3, w_out_3, ln_g_3, ln_b_3):
    n_b, seq, d_model = x.shape
    assert d_model == D_MODEL and seq % max(dl * BAND for _, dl in DILATED_GROUPS) == 0 and seq % TM == 0
    assert (seq // MOBA_BLOCK) % 8 == 0 and seq // MOBA_BLOCK <= LANES - HEAD_DIM
    layers = [(w_in_0, sink_0, w_out_0, ln_g_0, ln_b_0),
              (w_in_1, None, w_out_1, ln_g_1, ln_b_1),
              (w_in_2, None, w_out_2, ln_g_2, ln_b_2),
              (w_in_3, sink_3, w_out_3, ln_g_3, ln_b_3)]
    dil_b = [dl for _, dl in DILATED_GROUPS]
    rope_nat = _rope_base(seq)
    x2 = x.reshape(n_b * seq, D_MODEL)
    for i, (w_in, sink, w_out, ln_g, ln_b) in enumerate(layers):
        kind = i % N_MIXERS
        w = _prep_w_in(w_in, kind)
        nat = lambda a: a.reshape(n_b, seq, a.shape[-1])
        if kind == 1:
            tabs = [_rope_table(rope_nat, dl) for dl in dil_b]
            grp, z = _inproj(x2, tabs, w, n_b=n_b, seq=seq, dils=dil_b, n_kvh=KV_HEADS[kind], moba=False)
            outs, lses = [], []
            for gi, ((window, dl), (q, kx, vx)) in enumerate(zip(DILATED_GROUPS, grp)):
                o, lse = _band_attention(q, kx, vx, max_dist=window // dl, name="dilated_group_%d" % gi)
                outs.append(o)
                lses.append(lse)
            y = _dilated_merge(outs, lses, z, n_b=n_b, seq=seq, dils=dil_b)
        else:
            grp, z = _inproj(x2, [rope_nat], w, n_b=n_b, seq=seq, dils=[1], n_kvh=KV_HEADS[kind],
                             moba=(kind == 2))
            q, kx, vx = grp[0]
            if kind == 0:
                y = _band_attention(q, kx, vx, max_dist=WINDOW_A - 1, sink=sink,
                                    z=z.reshape(n_b, 1, seq, D_MODEL), name="swa_attention")
            else:
                y = _moba_attention(q, nat(kx), nat(vx), nat(z))
        x2 = _outproj(y.reshape(n_b * seq, D_MODEL), x2, w_out.astype(BF16),
                      ln_g.reshape(1, D_MODEL).astype(F32), ln_b.reshape(1, D_MODEL).astype(F32))
    return x2.reshape(n_b, seq, D_MODEL)
```

```python
import functools
import math

import numpy as np
import jax
import jax.numpy as jnp
from jax import lax
from jax.experimental import pallas as pl
from jax.experimental.pallas import tpu as pltpu

D_MODEL = 1024
HEAD_DIM = 64
N_HEADS = D_MODEL // HEAD_DIM
ROT_DIM = HEAD_DIM // 4
ROPE_THETA = 500000.0
DEPTH = 4
N_MIXERS = 3
KV_HEADS = (2, 4, 4)
WINDOW_A = 128
DILATED_GROUPS = ((128, 1), (512, 4), (2048, 16))
BAND = 128
MOBA_BLOCK = 256
MOBA_TOPK = 3
DEEPNORM_ALPHA = (2 * DEPTH) ** 0.25
LN_EPS = 1e-5
LOG2E = math.log2(math.e)
Q_SCALE = HEAD_DIM ** -0.5 * LOG2E

LANES = 128
LSE_LANES = LANES // N_HEADS
NEG_BIG = -1e30
VMEM_CAP = 60000 * 1024
TM = 512
MERGE_CHUNK = 256
BAND_TILE = 2048
MOBA_UNROLL = 4

F32 = jnp.float32
BF16 = jnp.bfloat16
NT_DIMS = (((1,), (1,)), ((), ()))


def _params(vmem_bytes, n_grid):
    limit = int(min(max(2 * vmem_bytes, 32 * 1024 * 1024), VMEM_CAP))
    return pltpu.CompilerParams(dimension_semantics=("arbitrary",) * n_grid,
                                vmem_limit_bytes=limit)


def _silu(z):
    return z / (1.0 + jnp.exp(-z))


def _fold_perm(n, dil):
    p = np.zeros((n, n), np.float32)
    idx = np.arange(n)
    p[(idx % dil) * (n // dil) + idx // dil, idx] = 1.0
    return p


def _moba_bias(km, qrows, t, n_blk):
    rows_n = qrows.shape[0]
    gate = lax.dot_general(km, qrows, NT_DIMS, preferred_element_type=F32)
    blk = lax.broadcasted_iota(jnp.int32, gate.shape, 0).astype(F32)
    tf = t.astype(F32)
    gate = jnp.where(blk < tf, gate, -jnp.inf)
    sel = jnp.zeros(gate.shape, F32)
    for _ in range(MOBA_TOPK):
        mx = jnp.max(gate, axis=0, keepdims=True)
        is_max = (gate == mx) & (mx > -jnp.inf)
        first = jnp.min(jnp.where(is_max, blk, float(n_blk)), axis=0, keepdims=True)
        pick = blk == first
        sel = jnp.where(pick, 1.0, sel)
        gate = jnp.where(pick, -jnp.inf, gate)
    bias_t = jnp.where((sel > 0.0) | (blk == tf), 0.0, NEG_BIG)
    bias_full = jnp.concatenate(
        [jnp.zeros((HEAD_DIM, rows_n), F32), bias_t,
         jnp.zeros((LANES - HEAD_DIM - n_blk, rows_n), F32)], axis=0)
    return bias_full.T.astype(BF16)


def _inproj_kernel(*refs, dils, n_kvh, moba, seq_tiles):
    n_grp = len(dils)
    kv = n_kvh * HEAD_DIM
    it = iter(refs)
    x_ref = next(it)
    groups = [(d, next(it) if d > 1 else None, next(it)) for d in dils]
    w_ref = next(it)
    outs = [(next(it), next(it), next(it)) for _ in range(n_grp)]
    z_ref = next(it)

    def proj(lhs, col, width):
        return jnp.dot(lhs, w_ref[:, col:col + width], preferred_element_type=F32)

    tm = x_ref.shape[0]
    xb = x_ref[...].astype(BF16)
    lane = lax.broadcasted_iota(jnp.int32, (tm, LANES), 1)
    lo = lane < HEAD_DIM
    if moba:
        km_scr = next(it)
        tile = pl.program_id(0) % seq_tiles
        n_q = tm // MOBA_BLOCK
        n_blk = seq_tiles * n_q
        row = lax.broadcasted_iota(jnp.int32, (tm, LANES), 0)
        pos = tile * tm + row
        k_aux = jnp.where(lane == HEAD_DIM + lax.shift_right_logical(pos, 8), 1.0, 0.0)

        @pl.when(tile == 0)
        def _new_sequence():
            km_scr[...] = jnp.zeros(km_scr.shape, F32)

    off = 0
    for (d, perm_ref, rope_ref), (q_ref, kx_ref, vx_ref) in zip(groups, outs):
        n = tm // d
        xg = xb if d == 1 else jnp.dot(perm_ref[...], xb, preferred_element_type=F32).astype(BF16)
        cs, sa, sb = rope_ref[0], rope_ref[1], rope_ref[2]

        def rope(t):
            return (t * cs + pltpu.roll(t, LANES - ROT_DIM // 2, 1) * sa
                    + pltpu.roll(t, ROT_DIM // 2, 1) * sb)

        def put(ref, col, val):
            for r in range(d):
                ref[r, :, col:col + val.shape[1]] = val[r * n:(r + 1) * n]

        acc_k = proj(xg, off + D_MODEL, kv)
        acc_v = proj(xg, off + D_MODEL + kv, kv)
        for c in range(n_kvh // 2):
            k2 = rope(acc_k[:, c * LANES:(c + 1) * LANES])
            v2 = acc_v[:, c * LANES:(c + 1) * LANES]
            k2r = pltpu.roll(k2, HEAD_DIM, 1)
            v2r = pltpu.roll(v2, HEAD_DIM, 1)
            for e in range(2):
                j = 2 * c + e
                if moba:
                    ke = jnp.where(lo, k2 if e == 0 else k2r, 0.0)
                    for bq in range(n_q):
                        mean = jnp.sum(ke[bq * MOBA_BLOCK:(bq + 1) * MOBA_BLOCK], axis=0,
                                       keepdims=True) * (1.0 / MOBA_BLOCK)
                        km_scr[j, pl.ds(HEAD_DIM + tile * n_q + bq, 1), :] = mean
                    kk = jnp.where(lo, ke, k_aux)
                    vv = jnp.where(lo, v2 if e == 0 else v2r, 1.0)
                else:
                    kk = jnp.where(lo, k2, k2r) if e == 0 else jnp.where(lo, k2r, k2)
                    vv = jnp.where(lo, v2, v2r) if e == 0 else jnp.where(lo, v2r, v2)
                put(kx_ref, j * LANES, kk.astype(BF16))
                put(vx_ref, j * LANES, vv.astype(BF16))

        for c in range(D_MODEL // 512):
            acc = proj(xg, off + c * 512, 512)
            heads = []
            for s in range(4):
                q2 = rope(acc[:, s * LANES:(s + 1) * LANES])
                if moba:
                    heads.append(jnp.where(lo, q2, 0.0).astype(BF16))
                    heads.append(jnp.where(lo, pltpu.roll(q2, HEAD_DIM, 1), 0.0).astype(BF16))
                else:
                    put(q_ref, c * 512 + s * LANES, q2.astype(BF16))
            if moba:
                g = N_HEADS // n_kvh
                for e in range(len(heads) // g):
                    j = c * (len(heads) // g) + e
                    km = km_scr[j, HEAD_DIM:HEAD_DIM + n_blk, :].astype(BF16)
                    for bq in range(n_q):
                        rows = slice(bq * MOBA_BLOCK, (bq + 1) * MOBA_BLOCK)
                        qrows = jnp.concatenate([heads[e * g + gi][rows] for gi in range(g)], axis=0)
                        bias = _moba_bias(km, qrows, tile * n_q + bq, n_blk)
                        q_ref[j, bq * g * MOBA_BLOCK:(bq + 1) * g * MOBA_BLOCK, :] = qrows + bias
        off += D_MODEL + 2 * kv

    for c in range(D_MODEL // 512):
        z_ref[:, c * 512:(c + 1) * 512] = proj(xb, off + c * 512, 512).astype(BF16)


def _inproj(x2, rope_tabs, w, *, n_b, seq, dils, n_kvh, moba):
    tm = TM if (moba or len(dils) > 1) else 2 * TM
    seq_tiles = seq // tm
    kw = n_kvh * LANES
    const = lambda a: pl.BlockSpec(a.shape, lambda i: (0, 0), pipeline_mode=pl.Buffered(1))
    in_specs = [pl.BlockSpec((tm, D_MODEL), lambda i: (i, 0))]
    args = [x2]
    for d, tab in zip(dils, rope_tabs):
        if d > 1:
            perm = jnp.asarray(_fold_perm(tm, d), BF16)
            in_specs.append(const(perm))
            args.append(perm)
        in_specs.append(pl.BlockSpec((3, tm, LANES), lambda i: (0, i % seq_tiles, 0)))
        args.append(tab)
    in_specs.append(const(w))
    args.append(w)
    out_specs, out_shape = [], []
    g = N_HEADS // n_kvh
    for d in dils:
        omap = lambda i: (i // seq_tiles, 0, i % seq_tiles, 0)
        for width in (D_MODEL, kw, kw):
            out_specs.append(pl.BlockSpec((None, d, tm // d, width), omap))
            out_shape.append(jax.ShapeDtypeStruct((n_b, d, seq // d, width), BF16))
    if moba:
        out_specs[0] = pl.BlockSpec((None, n_kvh, tm * g, LANES), omap)
        out_shape[0] = jax.ShapeDtypeStruct((n_b, n_kvh, seq * g, LANES), BF16)
    out_specs.append(pl.BlockSpec((tm, D_MODEL), lambda i: (i, 0)))
    out_shape.append(jax.ShapeDtypeStruct((n_b * seq, D_MODEL), BF16))
    scratch = [pltpu.VMEM((n_kvh, LANES, LANES), F32)] if moba else []
    vmem = (2 * tm * D_MODEL * 4 + len(dils) * (2 * 3 * tm * LANES * 4 + tm * tm * 2)
            + 2 * D_MODEL * w.shape[1] + 2 * 2 * tm * (len(dils) * (2 * D_MODEL + 2 * kw) + D_MODEL)
            + 6 * tm * 512 * 4)
    res = pl.pallas_call(
        functools.partial(_inproj_kernel, dils=tuple(dils), n_kvh=n_kvh, moba=moba, seq_tiles=seq_tiles),
        grid=(n_b * seq // tm,),
        in_specs=in_specs, out_specs=out_specs, out_shape=out_shape, scratch_shapes=scratch,
        compiler_params=_params(vmem, 1),
        name="inproj",
    )(*args)
    return [tuple(res[3 * g:3 * g + 3]) for g in range(len(dils))], res[-1]


def _outproj_kernel(y_ref, x_ref, w_ref, g_ref, b_ref, o_ref):
    rows = 256
    for c in range(y_ref.shape[0] // rows):
        sl = slice(c * rows, (c + 1) * rows)
        t = DEEPNORM_ALPHA * x_ref[sl, :] + jnp.dot(y_ref[sl, :], w_ref[...], preferred_element_type=F32)
        mu = jnp.mean(t, axis=-1, keepdims=True)
        d = t - mu
        var = jnp.mean(d * d, axis=-1, keepdims=True)
        o_ref[sl, :] = d * lax.rsqrt(var + LN_EPS) * g_ref[...] + b_ref[...]


def _outproj(y2, x2, w, g, b):
    t_rows = x2.shape[0]
    tm = 4 * TM
    row = lambda i: (i, 0)
    full = lambda i: (0, 0)
    vmem = 2 * tm * D_MODEL * (2 + 4 + 4) + 2 * D_MODEL * D_MODEL * 2 + 3 * 128 * D_MODEL * 4
    return pl.pallas_call(
        _outproj_kernel,
        grid=(t_rows // tm,),
        in_specs=[pl.BlockSpec((tm, D_MODEL), row), pl.BlockSpec((tm, D_MODEL), row),
                  pl.BlockSpec((D_MODEL, D_MODEL), full),
                  pl.BlockSpec((1, D_MODEL), full), pl.BlockSpec((1, D_MODEL), full)],
        out_specs=pl.BlockSpec((tm, D_MODEL), row),
        out_shape=jax.ShapeDtypeStruct((t_rows, D_MODEL), F32),
        compiler_params=_params(vmem, 1),
        name="outproj_ln",
    )(y2, x2, w, g, b)


def _band_consts(max_dist, sink):
    i = np.arange(BAND)[:, None]
    c = np.arange(2 * BAND)[None, :]
    dist = BAND + i - c
    ok = (dist >= 0) & (dist <= max_dist)
    first = ok & (c >= BAND)
    sel = np.zeros((2, 2 * BAND, LANES), np.float32)
    sel[0, :, :HEAD_DIM] = 1.0
    sel[1, :, HEAD_DIM:] = 1.0
    ones = sel.copy()
    if sink is not None:
        assert not ok[:, 0].any()
        ok[:, 0] = True
        first[:, 0] = True
        sel[:, 0, :] = 0.0
    mask = np.where(np.stack([first, ok]), 0.0, NEG_BIG)
    consts = [jnp.asarray(np.concatenate([mask, mask], axis=2), F32),
              jnp.asarray(sel, BF16), jnp.asarray(ones, BF16)]
    if sink is not None:
        s2 = sink.astype(F32) * LOG2E
        hi = s2.astype(BF16)
        lo = (s2 - hi.astype(F32)).astype(BF16)
        per_pair = jnp.stack([hi[0::2], lo[0::2], hi[1::2], lo[1::2]], axis=1)
        per_pair = jnp.pad(per_pair, ((0, 0), (0, LANES - 4)))
        k_aux = np.zeros((2, 2 * BAND, LANES), np.float32)
        k_aux[0, 0, 0:2] = 1.0
        k_aux[1, 0, 2:4] = 1.0
        consts += [jnp.repeat(per_pair, BAND, axis=0), jnp.asarray(k_aux, BF16)]
    return consts


def _band_kernel(*refs, n_kvh, has_sink):
    if has_sink:
        (mask_ref, sel_ref, ones_ref, qaux_ref, kaux_ref, q_ref, kc_ref, kp_ref, vc_ref, vp_ref, z_ref,
         y_ref, kbuf, vbuf) = refs
    else:
        mask_ref, sel_ref, ones_ref, q_ref, kc_ref, kp_ref, vc_ref, vp_ref, o_ref, lse_ref, kbuf, vbuf = refs
    n_pair = N_HEADS // n_kvh // 2
    n_seq, tq = q_ref.shape[0], q_ref.shape[1]
    i_tile = pl.program_id(2)
    kbuf[:, 0:BAND, :] = kp_ref[...]
    kbuf[:, BAND:BAND + tq, :] = kc_ref[...]
    vbuf[:, 0:BAND, :] = vp_ref[...]
    vbuf[:, BAND:BAND + tq, :] = vc_ref[...]
    lane = lax.broadcasted_iota(jnp.int32, (BAND, LANES), 1)
    lo = lane < HEAD_DIM
    mask_first = mask_ref[jnp.where(i_tile == 0, 0, 1)]
    mask_rest = mask_ref[1]

    for rr in range(n_seq):
        for bb in range(tq // BAND):
            r0 = bb * BAND
            mask = mask_first if bb == 0 else mask_rest
            lse_acc = jnp.zeros((BAND, LANES), F32)
            for j in range(n_kvh):
                kcat = kbuf[rr, r0:r0 + 2 * BAND, j * LANES:(j + 1) * LANES]
                vcat = vbuf[rr, r0:r0 + 2 * BAND, j * LANES:(j + 1) * LANES]
                k_e, k_o = kcat * sel_ref[0], kcat * sel_ref[1]
                v_e = jnp.concatenate([vcat * sel_ref[0], ones_ref[0]], axis=1)
                v_o = jnp.concatenate([vcat * sel_ref[1], ones_ref[1]], axis=1)
                qp = jnp.concatenate(
                    [q_ref[rr, r0:r0 + BAND, (j * n_pair + pi) * LANES:(j * n_pair + pi + 1) * LANES]
                     for pi in range(n_pair)], axis=0)
                if has_sink:
                    qp = jnp.concatenate([qp, qaux_ref[j * n_pair * BAND:(j + 1) * n_pair * BAND, :]], axis=1)
                    k_e = jnp.concatenate([k_e, kaux_ref[0]], axis=1)
                    k_o = jnp.concatenate([k_o, kaux_ref[1]], axis=1)
                kbd = jnp.concatenate([k_e, k_o], axis=0)
                vbd = jnp.concatenate([v_e, v_o], axis=0)
                s = lax.dot_general(qp, kbd, NT_DIMS, preferred_element_type=F32)
                ps, ms = [], []
                for pi in range(n_pair):
                    sp = s[pi * BAND:(pi + 1) * BAND] + mask
                    m_e = jnp.max(sp[:, 0:2 * BAND], axis=-1, keepdims=True)
                    m_o = jnp.max(sp[:, 2 * BAND:], axis=-1, keepdims=True)
                    m_full = jnp.concatenate([jnp.broadcast_to(m_e, (BAND, 2 * BAND)),
                                              jnp.broadcast_to(m_o, (BAND, 2 * BAND))], axis=1)
                    ms.append((m_e, m_o))
                    ps.append(jnp.exp2(sp - m_full).astype(BF16))
                p = jnp.concatenate(ps, axis=0)
                oext = jnp.dot(p, vbd, preferred_element_type=F32)
                for pi in range(n_pair):
                    oe = oext[pi * BAND:(pi + 1) * BAND]
                    o_slab = oe[:, 0:LANES] / oe[:, LANES:]
                    pair = j * n_pair + pi
                    col = pair * LANES
                    if has_sink:
                        zf = z_ref[rr, r0:r0 + BAND, col:col + LANES].astype(F32)
                        y_ref[rr, r0:r0 + BAND, col:col + LANES] = (o_slab * _silu(zf)).astype(BF16)
                    else:
                        o_ref[rr, r0:r0 + BAND, col:col + LANES] = o_slab.astype(BF16)
                        lse_slab = jnp.where(lo, ms[pi][0], ms[pi][1]) + jnp.log2(oe[:, LANES:])
                        a, b = pair * LSE_LANES, HEAD_DIM + pair * LSE_LANES
                        mine = ((lane >= a) & (lane < a + LSE_LANES)) | ((lane >= b) & (lane < b + LSE_LANES))
                        lse_acc = jnp.where(mine, lse_slab, lse_acc)
            if not has_sink:
                lse_ref[rr, r0:r0 + BAND, :] = lse_acc


def _band_attention(q, kx, vx, *, max_dist, sink=None, z=None, name):
    n_b, n_seq, length, _ = q.shape
    kw = kx.shape[3]
    n_kvh = kw // LANES
    tq = min(BAND_TILE, length)
    per_step = min(n_seq, BAND_TILE // tq)
    bpt = tq // BAND
    cur = lambda b, r, i: (b, r, i, 0)
    prev = lambda b, r, i: (b, r, jnp.maximum(i * bpt - 1, 0), 0)
    consts = _band_consts(max_dist, sink)
    const_specs = [pl.BlockSpec(c.shape, lambda b, r, i, n=c.ndim: (0,) * n) for c in consts]
    wide = pl.BlockSpec((None, per_step, tq, D_MODEL), cur)
    in_specs = const_specs + [
        wide,
        pl.BlockSpec((None, per_step, tq, kw), cur), pl.BlockSpec((None, per_step, BAND, kw), prev),
        pl.BlockSpec((None, per_step, tq, kw), cur), pl.BlockSpec((None, per_step, BAND, kw), prev)]
    args = [*consts, q, kx, kx, vx, vx]
    shape = (n_b, n_seq, length, D_MODEL)
    if sink is not None:
        in_specs.append(wide)
        args.append(z)
        out_specs, out_shape = wide, jax.ShapeDtypeStruct(shape, BF16)
    else:
        out_specs = [wide, pl.BlockSpec((None, per_step, tq, LANES), cur)]
        out_shape = [jax.ShapeDtypeStruct(shape, BF16),
                     jax.ShapeDtypeStruct((n_b, n_seq, length, LANES), F32)]
    rows = per_step * tq
    vmem = (2 * rows * D_MODEL * 2 * 3 + 2 * rows * LANES * 4 + 6 * per_step * (tq + BAND) * kw * 2 * 2
            + 2 * sum(int(np.prod(c.shape)) * c.dtype.itemsize for c in consts) + 16 * 4 * BAND * 4 * BAND * 4)
    return pl.pallas_call(
        functools.partial(_band_kernel, n_kvh=n_kvh, has_sink=sink is not None),
        grid=(n_b, n_seq // per_step, length // tq),
        in_specs=in_specs, out_specs=out_specs, out_shape=out_shape,
        scratch_shapes=[pltpu.VMEM((per_step, tq + BAND, kw), BF16), pltpu.VMEM((per_step, tq + BAND, kw), BF16)],
        compiler_params=_params(vmem, 3),
        name=name,
    )(*args)


def _dilated_merge_kernel(*refs, dils):
    n_grp = len(dils)
    it = iter(refs)
    o_refs = [next(it) for _ in range(n_grp)]
    l_refs = [next(it) for _ in range(n_grp)]
    p_refs = [next(it) if d > 1 else None for d in dils]
    e_ref, z_ref, y_ref = next(it), next(it), next(it)
    tc = MERGE_CHUNK

    def split3(a):
        hi = a.astype(BF16)
        r1 = a - hi.astype(F32)
        mid = r1.astype(BF16)
        return hi, mid, (r1 - mid.astype(F32)).astype(BF16)

    for c in range(z_ref.shape[0] // tc):
        outs, lses = [], []
        for d, o_ref, l_ref, p_ref in zip(dils, o_refs, l_refs, p_refs):
            fold = slice(c * tc // d, (c + 1) * tc // d)
            o = o_ref[:, fold, :].reshape(tc, D_MODEL)
            lse = l_ref[:, fold, :].reshape(tc, LANES)
            if d > 1:
                pt = p_ref[...]
                o = jnp.dot(pt, o, preferred_element_type=F32)
                parts = jnp.concatenate(split3(lse), axis=1)
                lp = jnp.dot(pt, parts, preferred_element_type=F32)
                lse = lp[:, 0:LANES] + lp[:, LANES:2 * LANES] + lp[:, 2 * LANES:]
            else:
                o = o.astype(F32)
            outs.append(o)
            lses.append(lse)
        m = functools.reduce(jnp.maximum, lses)
        es = [jnp.exp2(l - m) for l in lses]
        den = functools.reduce(lambda a, b: a + b, es)
        acc = jnp.zeros((tc, D_MODEL), F32)
        for e, o in zip(es, outs):
            hi, mid, _ = split3(e / den)
            w = jnp.dot(jnp.concatenate([hi, mid], axis=1), e_ref[...], preferred_element_type=F32)
            acc = acc + w * o
        rows = slice(c * tc, (c + 1) * tc)
        y_ref[rows, :] = (acc * _silu(z_ref[rows, :].astype(F32))).astype(BF16)


def _dilated_merge(outs, lses, z2, *, n_b, seq, dils):
    tm = 2 * MERGE_CHUNK
    seq_tiles = seq // tm
    expand = np.zeros((LANES, D_MODEL), np.float32)
    for h in range(N_HEADS):
        expand[(h % 2) * HEAD_DIM + (h // 2) * LSE_LANES, h * HEAD_DIM:(h + 1) * HEAD_DIM] = 1.0
    expand = jnp.asarray(np.concatenate([expand, expand], axis=0), BF16)
    omap = lambda i: (i // seq_tiles, 0, i % seq_tiles, 0)
    in_specs = [pl.BlockSpec((None, d, tm // d, D_MODEL), omap) for d in dils]
    in_specs += [pl.BlockSpec((None, d, tm // d, LANES), omap) for d in dils]
    perms = [jnp.asarray(_fold_perm(MERGE_CHUNK, d).T, BF16) for d in dils if d > 1]
    in_specs += [pl.BlockSpec(p.shape, lambda i: (0, 0)) for p in perms]
    in_specs += [pl.BlockSpec(expand.shape, lambda i: (0, 0)), pl.BlockSpec((tm, D_MODEL), lambda i: (i, 0))]
    vmem = (2 * tm * D_MODEL * (3 * 2 + 2 + 2) + 2 * 3 * tm * LANES * 4 + 4 * tm * tm * 2
            + 2 * 2 * LANES * D_MODEL * 2 + 10 * tm * D_MODEL * 4)
    return pl.pallas_call(
        functools.partial(_dilated_merge_kernel, dils=tuple(dils)),
        grid=(n_b * seq // tm,),
        in_specs=in_specs,
        out_specs=pl.BlockSpec((tm, D_MODEL), lambda i: (i, 0)),
        out_shape=jax.ShapeDtypeStruct((n_b * seq, D_MODEL), BF16),
        compiler_params=_params(vmem, 1),
        name="dilated_merge",
    )(*outs, *lses, *perms, expand, z2)


def _moba_kernel(tril_ref, qext_ref, kx_ref, vx_ref, z_ref, y_ref, sa_scr, sb_scr, m_scr, acc_scr):
    t = pl.program_id(2)
    g = 4
    tq = MOBA_BLOCK
    rows_n = g * tq
    lane = lax.broadcasted_iota(jnp.int32, (tq, LANES), 1)
    lo = lane < HEAD_DIM

    def block(ref, blk_idx):
        r0 = pl.multiple_of(blk_idx * MOBA_BLOCK, MOBA_BLOCK)
        return ref[pl.ds(r0, MOBA_BLOCK), :]

    def scores(blk_idx):
        return lax.dot_general(qext_ref[...], block(kx_ref, blk_idx), NT_DIMS, preferred_element_type=F32)

    def blocks_of(j):
        first = jnp.where(j == 0, t, 2 * j - 1)
        second = jnp.where(2 * j + 1 <= t, 2 * j, t + 1)
        return first, second

    def produce(dst, j):
        first, second = blocks_of(j)
        dst[:, 0:tq] = scores(first)
        dst[:, tq:2 * tq] = scores(second)

    def consume(src, j):
        first, second = blocks_of(j)
        sn = src[...]
        m_old = m_scr[...]
        m_new = jnp.maximum(m_old, jnp.max(sn, axis=-1, keepdims=True))
        alpha = jnp.exp2(m_old - m_new)
        pn = jnp.exp2(sn - jnp.concatenate([m_new] * (2 * tq // LANES), axis=1)).astype(BF16)
        acc_scr[...] = (alpha * acc_scr[...]
                        + jnp.dot(pn[:, 0:tq], block(vx_ref, first), preferred_element_type=F32)
                        + jnp.dot(pn[:, tq:2 * tq], block(vx_ref, second), preferred_element_type=F32))
        m_scr[...] = m_new

    n_items = (t + 2) // 2
    tril = tril_ref[...]
    s0 = scores(t)
    sa_scr[:, 0:tq] = jnp.concatenate([s0[gi * tq:(gi + 1) * tq] + tril for gi in range(g)], axis=0)
    sa_scr[:, tq:2 * tq] = scores(blocks_of(0)[1])
    m_scr[...] = jnp.full((rows_n, LANES), NEG_BIG, F32)
    acc_scr[...] = jnp.zeros((rows_n, LANES), F32)

    def stage(src, dst, j):
        if dst is not None:
            produce(dst, jnp.minimum(j + 1, n_items - 1))
        consume(src, j)

    bufs = (sa_scr, sb_scr)

    def group(gg, carry):
        for u in range(MOBA_UNROLL):
            stage(bufs[u % 2], bufs[(u + 1) % 2], MOBA_UNROLL * gg + u)
        return carry

    lax.fori_loop(0, n_items // MOBA_UNROLL, group, 0)
    rem = n_items % MOBA_UNROLL
    base = n_items - rem
    for u in range(MOBA_UNROLL - 1):
        if u < MOBA_UNROLL - 2:
            pl.when(rem > u + 1)(functools.partial(stage, bufs[u % 2], bufs[(u + 1) % 2], base + u))
        pl.when(rem == u + 1)(functools.partial(stage, bufs[u % 2], None, base + u))

    acc = acc_scr[...]
    for pi in range(g // 2):
        oe = acc[(2 * pi) * tq:(2 * pi + 1) * tq]
        oo = acc[(2 * pi + 1) * tq:(2 * pi + 2) * tq]
        o_slab = jnp.where(lo, oe / pltpu.roll(oe, HEAD_DIM, 1), pltpu.roll(oo, HEAD_DIM, 1) / oo)
        zf = z_ref[:, pi * LANES:(pi + 1) * LANES].astype(F32)
        y_ref[:, pi * LANES:(pi + 1) * LANES] = (o_slab * _silu(zf)).astype(BF16)


def _moba_attention(qext, kx, vx, z):
    n_b, seq, _ = z.shape
    n_kvh = kx.shape[2] // LANES
    g = N_HEADS // n_kvh
    tq = MOBA_BLOCK
    qw = g * HEAD_DIM
    tril = jnp.asarray(np.where(np.tril(np.ones((tq, tq), bool)), 0.0, NEG_BIG), F32)
    tile = lambda b, j, t: (b, t, j)
    whole = lambda b, j, t: (b, 0, j)
    vmem = (2 * 2 * seq * LANES * 2 + 2 * 2 * tq * qw * 2 + tq * tq * 4 * 2
            + g * tq * LANES * (2 * 2 + 4 + 4) + 2 * g * tq * 2 * tq * 4 + 6 * g * tq * 2 * tq * 4)
    return pl.pallas_call(
        _moba_kernel,
        grid=(n_b, n_kvh, seq // tq),
        in_specs=[pl.BlockSpec((tq, tq), lambda b, j, t: (0, 0)),
                  pl.BlockSpec((None, None, g * tq, LANES), lambda b, j, t: (b, j, t, 0)),
                  pl.BlockSpec((None, seq, LANES), whole), pl.BlockSpec((None, seq, LANES), whole),
                  pl.BlockSpec((None, tq, qw), tile)],
        out_specs=pl.BlockSpec((None, tq, qw), tile),
        out_shape=jax.ShapeDtypeStruct((n_b, seq, D_MODEL), BF16),
        scratch_shapes=[pltpu.VMEM((g * tq, 2 * tq), F32), pltpu.VMEM((g * tq, 2 * tq), F32),
                        pltpu.VMEM((g * tq, LANES), F32), pltpu.VMEM((g * tq, LANES), F32)],
        compiler_params=_params(vmem, 3),
        name="moba_attention",
    )(tril, qext, kx, vx, z)


def _rope_base(seq):
    half = ROT_DIM // 2
    inv = ROPE_THETA ** (-jnp.arange(0, ROT_DIM, 2, dtype=F32) / ROT_DIM)
    c = np.arange(LANES) % HEAD_DIM
    rot = c < ROT_DIM
    freq = jnp.where(rot, inv[c % half], 0.0)
    ang = jnp.arange(seq, dtype=F32)[:, None] * freq[None, :]
    cos, sin = jnp.cos(ang), jnp.sin(ang)
    first = jnp.asarray(c < half, F32)
    second = jnp.asarray(rot & (c >= half), F32)
    return jnp.stack([cos, -sin * first, sin * second])


def _rope_table(tab, dil):
    if dil == 1:
        return tab
    seq = tab.shape[1]
    return tab.reshape(3, seq // TM, TM // dil, dil, LANES).transpose(0, 1, 3, 2, 4).reshape(3, seq, LANES)


def _prep_w_in(w_in, kind):
    kv = KV_HEADS[kind] * HEAD_DIM
    n_grp = len(DILATED_GROUPS) if kind == 1 else 1
    scale = np.ones((w_in.shape[1],), np.float32)
    for g in range(n_grp):
        off = g * (D_MODEL + 2 * kv)
        scale[off:off + D_MODEL] = Q_SCALE
    return (w_in * jnp.asarray(scale)).astype(BF16)


def kernel(x, w_in_0, sink_0, w_out_0, ln_g_0, ln_b_0, w_in_1, w_out_1, ln_g_1, ln_b_1, w_in_2, w_out_2, ln_g_2, ln_b_2, w_in_3, sink_3, w_out_3, ln_g_3, ln_b_3):
    n_b, seq, d_model = x.shape
    assert d_model == D_MODEL and seq % max(dl * BAND for _, dl in DILATED_GROUPS) == 0 and seq % TM == 0
    assert (seq // MOBA_BLOCK) % 8 == 0 and seq // MOBA_BLOCK <= LANES - HEAD_DIM
    layers = [(w_in_0, sink_0, w_out_0, ln_g_0, ln_b_0),
              (w_in_1, None, w_out_1, ln_g_1, ln_b_1),
              (w_in_2, None, w_out_2, ln_g_2, ln_b_2),
              (w_in_3, sink_3, w_out_3, ln_g_3, ln_b_3)]
    dil_b = [dl for _, dl in DILATED_GROUPS]
    rope_nat = _rope_base(seq)
    x2 = x.reshape(n_b * seq, D_MODEL)
    for i, (w_in, sink, w_out, ln_g, ln_b) in enumerate(layers):
        kind = i % N_MIXERS
        w = _prep_w_in(w_in, kind)
        nat = lambda a: a.reshape(n_b, seq, a.shape[-1])
        if kind == 1:
            tabs = [_rope_table(rope_nat, dl) for dl in dil_b]
            grp, z = _inproj(x2, tabs, w, n_b=n_b, seq=seq, dils=dil_b, n_kvh=KV_HEADS[kind], moba=False)
            outs, lses = [], []
            for gi, ((window, dl), (q, kx, vx)) in enumerate(zip(DILATED_GROUPS, grp)):
                o, lse = _band_attention(q, kx, vx, max_dist=window // dl, name="dilated_group_%d" % gi)
                outs.append(o)
                lses.append(lse)
            y = _dilated_merge(outs, lses, z, n_b=n_b, seq=seq, dils=dil_b)
        else:
            grp, z = _inproj(x2, [rope_nat], w, n_b=n_b, seq=seq, dils=[1], n_kvh=KV_HEADS[kind],
                             moba=(kind == 2))
            q, kx, vx = grp[0]
            if kind == 0:
                y = _band_attention(q, kx, vx, max_dist=WINDOW_A - 1, sink=sink,
                                    z=z.reshape(n_b, 1, seq, D_MODEL), name="swa_attention")
            else:
                y = _moba_attention(q, nat(kx), nat(vx), nat(z))
        x2 = _outproj(y.reshape(n_b * seq, D_MODEL), x2, w_out.astype(BF16),
                      ln_g.reshape(1, D_MODEL).astype(F32), ln_b.reshape(1, D_MODEL).astype(F32))
    return x2.reshape(n_b, seq, D_MODEL)
```
